```python
import jax, jax.numpy as jnp
from jax import lax
import numpy as np

D_MODEL = 4096
BATCH = 16
SEQ = 256
DEPTH = 4
DEC_BATCH = 4
DEC_SEQ = 4096
PAST_LEN = 256

GRID_W = 64
D_HG = D_MODEL // 2
HG_HEADS = 16
HG_DK = D_HG // HG_HEADS
HG_DV = D_HG // HG_HEADS
D_CV = D_MODEL // 2
CONV_W = 31
CHUNK = 32
EPS = 1e-6

IN_SIZES = (D_HG, D_HG, D_HG, D_HG, D_HG, D_CV, D_CV, D_CV, D_MODEL, D_MODEL)
IN_SPLITS = tuple(int(v) for v in np.cumsum(IN_SIZES)[:-1])
N_IN = int(sum(IN_SIZES))

kernel_name = "hgrn2_conformer_adaln_prefix_trunk"

F32 = jnp.float32


def _rmsnorm(x, g):
    x32 = x.astype(F32)
    y = x32 * lax.rsqrt(jnp.mean(x32 * x32, axis=-1, keepdims=True) + EPS)
    return (y * g.astype(F32)).astype(x.dtype)


def _layernorm(x, g, b):
    x32 = x.astype(F32)
    mu = jnp.mean(x32, axis=-1, keepdims=True)
    xc = x32 - mu
    y = xc * lax.rsqrt(jnp.mean(xc * xc, axis=-1, keepdims=True) + EPS)
    return (y * g.astype(F32) + b.astype(F32)).astype(x.dtype)


def _lower_bounds(lb_logits):
    p = jax.nn.softmax(lb_logits.astype(F32), axis=0)
    return jnp.cumsum(p, axis=0) - p[0:1]


def _chunk_scan(q, k, v, logf, s0):
    B, L, H, K = q.shape
    V = v.shape[-1]
    n = L // CHUNK

    def to_chunks(a):
        return a.astype(F32).reshape(B, n, CHUNK, H, a.shape[-1]).transpose(1, 0, 3, 2, 4)

    qs, ks, vs, gs = to_chunks(q), to_chunks(k), to_chunks(v), to_chunks(logf)
    causal = jnp.tril(jnp.ones((CHUNK, CHUNK), dtype=bool))

    def step(S, inp):
        qc, kc, vc, gc = inp
        b = jnp.cumsum(gc, axis=2)
        o_inter = jnp.einsum('bhtk,bhkv->bhtv', qc * jnp.exp(b), S)
        diff = b[:, :, :, None, :] - b[:, :, None, :, :]
        dec = jnp.exp(jnp.where(causal[:, :, None], diff, -jnp.inf))
        scores = jnp.einsum('bhtk,bhtsk,bhsk->bhts', qc, dec, kc)
        o_intra = jnp.einsum('bhts,bhsv->bhtv', scores, vc)
        b_end = b[:, :, -1, :]
        S_new = jnp.exp(b_end)[..., None] * S + jnp.einsum(
            'bhsk,bhsv->bhkv', kc * jnp.exp(b_end[:, :, None, :] - b), vc)
        return S_new, o_inter + o_intra

    S_fin, o = lax.scan(step, s0.astype(F32), (qs, ks, vs, gs))
    o = o.transpose(1, 0, 3, 2, 4).reshape(B, L, H, V)
    return o, S_fin


def _hgrn2(q_raw, ff_raw, fb_raw, i_raw, g_raw, lb, norm_g, s0_f, s0_b):
    B, L, _ = q_raw.shape

    def heads(a):
        return a.reshape(B, L, HG_HEADS, -1)

    q = heads(jax.nn.silu(q_raw.astype(F32)))
    v = heads(i_raw.astype(F32))

    def gates(z, lbd):
        z = z.astype(F32)
        logf = jnp.logaddexp(jnp.log(lbd), jnp.log1p(-lbd) + jax.nn.log_sigmoid(z))
        k = (1.0 - lbd) * jax.nn.sigmoid(-z)
        return heads(k), heads(logf)

    kf, gf = gates(ff_raw, lb[0])
    kb, gb = gates(fb_raw, lb[1])
    o_f, S_f = _chunk_scan(q, kf, v, gf, s0_f)
    o_b, S_b = _chunk_scan(q[:, ::-1], kb[:, ::-1], v[:, ::-1], gb[:, ::-1], s0_b)
    o = o_f + o_b[:, ::-1]
    o = o * lax.rsqrt(jnp.mean(o * o, axis=-1, keepdims=True) + EPS) * norm_g.astype(F32)
    o = o.reshape(B, L, D_HG) * jax.nn.silu(g_raw.astype(F32))
    return o.astype(q_raw.dtype), S_f, S_b


def _dwconv(u, w, b):
    y = lax.conv_general_dilated(
        u, w[:, None, :].astype(u.dtype), window_strides=(1,),
        padding=[(CONV_W // 2, CONV_W // 2)],
        dimension_numbers=('NWC', 'WIO', 'NWC'),
        feature_group_count=u.shape[-1])
    return y + b.astype(u.dtype)


def _conformer_conv(a, b_glu, g_raw, w_dw, b_dw, ln_g, ln_b, grid_mode):
    u = a * jax.nn.sigmoid(b_glu)
    B, L, C = u.shape
    if grid_mode is None:
        y = _dwconv(u, w_dw, b_dw)
    else:
        rows = L // GRID_W
        grid = u.reshape(B, rows, GRID_W, C)
        if grid_mode == 'row':
            y = _dwconv(grid.reshape(B * rows, GRID_W, C), w_dw, b_dw).reshape(B, L, C)
        else:
            lines = grid.transpose(0, 2, 1, 3).reshape(B * GRID_W, rows, C)
            y = _dwconv(lines, w_dw, b_dw).reshape(B, GRID_W, rows, C)
            y = y.transpose(0, 2, 1, 3).reshape(B, L, C)
    y = _layernorm(y, ln_g, ln_b)
    return jax.nn.silu(y) * jax.nn.silu(g_raw)


def _layer(x, mod, s0_f, s0_b, lb, grid_mode, norm_g, w_in, hg_norm_g, w_dw, b_dw,
           ln_g, ln_b, w_hproj, w_cproj, w_out):
    shift, scale, gate = jnp.split(mod.astype(x.dtype), 3, axis=-1)
    h = _rmsnorm(x, norm_g) * (1 + scale[:, None]) + shift[:, None]
    p = jnp.einsum('bld,dn->bln', h, w_in)
    q_raw, ff, fb, i_raw, g_hg, cv_a, cv_b, g_cv, m_hg, m_cv = jnp.split(p, IN_SPLITS, axis=-1)
    o_hg, S_f, S_b = _hgrn2(q_raw, ff, fb, i_raw, g_hg, lb, hg_norm_g, s0_f, s0_b)
    o_cv = _conformer_conv(cv_a, cv_b, g_cv, w_dw, b_dw, ln_g, ln_b, grid_mode)
    y_hg = jnp.einsum('blc,cd->bld', o_hg, w_hproj)
    y_cv = jnp.einsum('blc,cd->bld', o_cv, w_cproj)
    merged = jax.nn.sigmoid(m_hg) * y_hg + jax.nn.sigmoid(m_cv) * y_cv
    out = jnp.einsum('bld,de->ble', merged, w_out)
    return x + gate[:, None] * out, S_f, S_b


def setup_inputs(seed: int = 0) -> dict:
    key = jax.random.key(seed)
    ks = jax.random.split(key, 20)
    nrm = jax.random.normal
    D = D_MODEL
    return {
        "x_prompt": nrm(ks[0], (BATCH, SEQ, D), F32),
        "x_sample": nrm(ks[1], (DEC_BATCH, DEC_SEQ, D), F32),
        "state_hgrn": 0.5 * nrm(ks[2], (DEC_BATCH, DEPTH, 2, HG_HEADS, HG_DK, HG_DV), F32),
        "c": nrm(ks[3], (DEC_BATCH, D), F32),
        "c_ctx": nrm(ks[4], (D,), F32),
        "mod_w": 0.5 * D ** -0.5 * nrm(ks[5], (DEPTH, D, 3 * D), F32),
        "mod_b": 0.02 * nrm(ks[6], (DEPTH, 3 * D), F32),
        "norm_g": 1.0 + 0.05 * nrm(ks[7], (DEPTH, D), F32),
        "w_in": D ** -0.5 * nrm(ks[8], (DEPTH, D, N_IN), F32),
        "hg_lb_logits": nrm(ks[9], (DEPTH, 2, D_HG), F32),
        "hg_norm_g": 1.0 + 0.05 * nrm(ks[10], (DEPTH, HG_DV), F32),
        "cv_dw_w": CONV_W ** -0.5 * nrm(ks[11], (DEPTH, CONV_W, D_CV), F32),
        "cv_dw_b": 0.02 * nrm(ks[12], (DEPTH, D_CV), F32),
        "cv_ln_g": 1.0 + 0.05 * nrm(ks[13], (DEPTH, D_CV), F32),
        "cv_ln_b": 0.02 * nrm(ks[14], (DEPTH, D_CV), F32),
        "w_hproj": D_HG ** -0.5 * nrm(ks[15], (DEPTH, D_HG, D), F32),
        "w_cproj": D_CV ** -0.5 * nrm(ks[16], (DEPTH, D_CV, D), F32),
        "w_out": D ** -0.5 * nrm(ks[17], (DEPTH, D, D), F32),
        "final_g": 1.0 + 0.05 * nrm(ks[18], (D,), F32),
    }


def reference(x_prompt, x_sample, state_hgrn, c, c_ctx, mod_w, mod_b, norm_g, w_in,
              hg_lb_logits, hg_norm_g, cv_dw_w, cv_dw_b, cv_ln_g, cv_ln_b,
              w_hproj, w_cproj, w_out, final_g):
    lbs = _lower_bounds(hg_lb_logits)
    silu_ctx = jax.nn.silu(c_ctx.astype(F32))
    silu_c = jax.nn.silu(c.astype(F32))

    xp = x_prompt
    zeros = jnp.zeros((x_prompt.shape[0], HG_HEADS, HG_DK, HG_DV), F32)
    states = []
    for l in range(DEPTH):
        mod_ctx = (silu_ctx @ mod_w[l].astype(F32) + mod_b[l].astype(F32))[None]
        xp, S_f, S_b = _layer(xp, mod_ctx, zeros, zeros, lbs[l], None, norm_g[l], w_in[l],
                              hg_norm_g[l], cv_dw_w[l], cv_dw_b[l], cv_ln_g[l], cv_ln_b[l],
                              w_hproj[l], w_cproj[l], w_out[l])
        states.append(jnp.stack([S_f, S_b], axis=1))
    new_state_hgrn = jnp.stack(states, axis=1)

    xs = x_sample
    for l in range(DEPTH):
        mod = silu_c @ mod_w[l].astype(F32) + mod_b[l].astype(F32)
        grid_mode = 'row' if l % 2 == 0 else 'col'
        xs, _, _ = _layer(xs, mod, state_hgrn[:, l, 0], state_hgrn[:, l, 1], lbs[l], grid_mode,
                          norm_g[l], w_in[l], hg_norm_g[l], cv_dw_w[l], cv_dw_b[l],
                          cv_ln_g[l], cv_ln_b[l], w_hproj[l], w_cproj[l], w_out[l])

    y_prompt = _rmsnorm(xp, final_g)
    y_sample = _rmsnorm(xs, final_g)
    return (y_prompt, y_sample, new_state_hgrn)
```

```python
import functools

import jax
import jax.numpy as jnp
from jax import lax
from jax.experimental import pallas as pl
from jax.experimental.pallas import tpu as pltpu

F32 = jnp.float32
BF16 = jnp.bfloat16

EPS = 1e-6
HEAD_DIM = 128
CONV_TAPS = 31
CONV_HALF = CONV_TAPS // 2
CONV_ROWS = 64
LATENT_GRID_W = 64
SCAN_CHUNK = 256
SCAN_DIAG = 8
SCAN_HEADS_PER_STEP = 2
LANES = 128
SUBLANES = 8
VMEM_LIMIT_BYTES = 56 * 1024 * 1024


def _params(n_axes):
    return pltpu.CompilerParams(dimension_semantics=("arbitrary",) * n_axes,
                                vmem_limit_bytes=VMEM_LIMIT_BYTES)


def _sigmoid(x):
    return 1.0 / (1.0 + jnp.exp(-x))


def _silu(x):
    return x * _sigmoid(x)


def _lb_kernel(logit_ref, out_ref):
    depth = logit_ref.shape[0]
    x = logit_ref[...]
    mx = jnp.max(x, axis=0, keepdims=True)
    ex = jnp.exp(x - mx)
    p = ex / jnp.sum(ex, axis=0, keepdims=True)
    p0 = p[0:1]
    cum = p0
    for l in range(depth):
        if l > 0:
            cum = cum + p[l:l + 1]
        lb = cum - p0
        out_ref[0, l:l + 1, :] = jnp.log(lb)
        out_ref[1, l:l + 1, :] = jnp.log1p(-lb)
        out_ref[2, l:l + 1, :] = 1.0 - lb


def _lower_bound_terms(lb_logits):
    depth = lb_logits.shape[0]
    flat = lb_logits.reshape(depth, -1).astype(F32)
    return pl.pallas_call(
        _lb_kernel,
        out_shape=jax.ShapeDtypeStruct((3, depth, flat.shape[1]), F32),
        name="lb_terms",
    )(flat)


def _mod_kernel(c_ref, w_ref, b_ref, o_ref):
    a = _silu(c_ref[...]).astype(BF16)
    w = w_ref[...].astype(BF16)
    o_ref[...] = jnp.dot(a, w, preferred_element_type=F32) + b_ref[...]


def _modulation(cond, mod_w, mod_b, tn=512):
    depth, d, n = mod_w.shape
    r = cond.shape[0]
    return pl.pallas_call(
        _mod_kernel,
        grid=(depth, n // tn),
        in_specs=[pl.BlockSpec((r, d), lambda l, j: (0, 0)),
                  pl.BlockSpec((None, d, tn), lambda l, j: (l, 0, j)),
                  pl.BlockSpec((None, 1, tn), lambda l, j: (l, 0, j))],
        out_specs=pl.BlockSpec((None, r, tn), lambda l, j: (l, 0, j)),
        out_shape=jax.ShapeDtypeStruct((depth, r, n), F32),
        compiler_params=_params(2),
        name="adaln_mod",
    )(cond, mod_w, mod_b.reshape(depth, 1, n))


def _prenorm_kernel(x_ref, g_ref, mod_ref, h_ref):
    d = x_ref.shape[-1]
    x = x_ref[...]
    y = x * lax.rsqrt(jnp.mean(x * x, axis=-1, keepdims=True) + EPS) * g_ref[...]
    shift = mod_ref[:, 0:d]
    scale = mod_ref[:, d:2 * d]
    h_ref[...] = (y * (1.0 + scale) + shift).astype(h_ref.dtype)


def _prenorm(x, g, mod, row_of_batch, tl=256):
    b, l, d = x.shape
    return pl.pallas_call(
        _prenorm_kernel,
        grid=(b, l // tl),
        in_specs=[pl.BlockSpec((None, tl, d), lambda i, j: (i, j, 0)),
                  pl.BlockSpec((1, d), lambda i, j: (0, 0)),
                  pl.BlockSpec((None, 1, 3 * d), lambda i, j: (row_of_batch(i), 0, 0))],
        out_specs=pl.BlockSpec((None, tl, d), lambda i, j: (i, j, 0)),
        out_shape=jax.ShapeDtypeStruct((b, l, d), BF16),
        compiler_params=_params(2),
        name="prenorm",
    )(x, g.reshape(1, d), mod)


def _final_norm_kernel(x_ref, g_ref, o_ref):
    x = x_ref[...]
    o_ref[...] = x * lax.rsqrt(jnp.mean(x * x, axis=-1, keepdims=True) + EPS) * g_ref[...]


def _final_norm(x, g, tl=256):
    b, l, d = x.shape
    return pl.pallas_call(
        _final_norm_kernel,
        grid=(b, l // tl),
        in_specs=[pl.BlockSpec((None, tl, d), lambda i, j: (i, j, 0)),
                  pl.BlockSpec((1, d), lambda i, j: (0, 0))],
        out_specs=pl.BlockSpec((None, tl, d), lambda i, j: (i, j, 0)),
        out_shape=jax.ShapeDtypeStruct((b, l, d), F32),
        compiler_params=_params(2),
        name="final_norm",
    )(x, g.reshape(1, d))


def _matmul_kernel(a_ref, w_ref, o_ref):
    o_ref[...] = jnp.dot(a_ref[...], w_ref[...], preferred_element_type=F32).astype(o_ref.dtype)


def _in_proj(h, w, layer, tm=1024, tn=1024):
    m, k = h.shape
    n = w.shape[2]
    tm = min(tm, m)
    return pl.pallas_call(
        _matmul_kernel,
        grid=(m // tm, n // tn),
        in_specs=[pl.BlockSpec((tm, k), lambda i, j: (i, 0)),
                  pl.BlockSpec((None, k, tn), lambda i, j: (layer, 0, j))],
        out_specs=pl.BlockSpec((tm, tn), lambda i, j: (i, j)),
        out_shape=jax.ShapeDtypeStruct((m, n), F32),
        compiler_params=_params(2),
        name="in_proj",
    )(h, w)


def _merge_kernel(oh_ref, wh_ref, oc_ref, wc_ref, mh_ref, mc_ref, o_ref):
    yh = jnp.dot(oh_ref[...], wh_ref[...], preferred_element_type=F32)
    yc = jnp.dot(oc_ref[...], wc_ref[...], preferred_element_type=F32)
    o_ref[...] = (_sigmoid(mh_ref[...]) * yh + _sigmoid(mc_ref[...]) * yc).astype(o_ref.dtype)


def _merge_proj(o_hg, w_h, o_cv, w_c, layer, p, gate_col0, tm=1024, tn=512):
    m, kh = o_hg.shape
    kc = o_cv.shape[1]
    n = w_h.shape[2]
    tm = min(tm, m)
    gh = gate_col0 // tn
    gc = (gate_col0 + n) // tn
    return pl.pallas_call(
        _merge_kernel,
        grid=(m // tm, n // tn),
        in_specs=[pl.BlockSpec((tm, kh), lambda i, j: (i, 0)),
                  pl.BlockSpec((None, kh, tn), lambda i, j: (layer, 0, j)),
                  pl.BlockSpec((tm, kc), lambda i, j: (i, 0)),
                  pl.BlockSpec((None, kc, tn), lambda i, j: (layer, 0, j)),
                  pl.BlockSpec((tm, tn), lambda i, j: (i, gh + j)),
                  pl.BlockSpec((tm, tn), lambda i, j: (i, gc + j))],
        out_specs=pl.BlockSpec((tm, tn), lambda i, j: (i, j)),
        out_shape=jax.ShapeDtypeStruct((m, n), BF16),
        compiler_params=_params(2),
        name="merge_proj",
    )(o_hg, w_h, o_cv, w_c, p, p)


def _out_kernel(a_ref, w_ref, x_ref, gate_ref, o_ref):
    y = jnp.dot(a_ref[...], w_ref[...], preferred_element_type=F32)
    o_ref[...] = x_ref[...] + gate_ref[...] * y


def _out_proj(merged, w, layer, x, mod, rows_per_batch, row_of_batch, tm=1024, tn=512):
    m, k = merged.shape
    d = w.shape[2]
    tm = min(tm, m, rows_per_batch)
    blocks_per_batch = rows_per_batch // tm
    gate0 = 2 * d // tn
    return pl.pallas_call(
        _out_kernel,
        grid=(m // tm, d // tn),
        in_specs=[pl.BlockSpec((tm, k), lambda i, j: (i, 0)),
                  pl.BlockSpec((None, k, tn), lambda i, j: (layer, 0, j)),
                  pl.BlockSpec((tm, tn), lambda i, j: (i, j)),
                  pl.BlockSpec((None, 1, tn),
                               lambda i, j: (row_of_batch(i // blocks_per_batch), 0, gate0 + j))],
        out_specs=pl.BlockSpec((tm, tn), lambda i, j: (i, j)),
        out_shape=jax.ShapeDtypeStruct((m, d), F32),
        compiler_params=_params(2),
        name="out_proj",
    )(merged, w, x, mod)


def _dot_nt(a, b):
    return lax.dot_general(a, b, (((1,), (1,)), ((), ())), preferred_element_type=F32)


def _cumsum_rows(x, reverse):
    n = x.shape[0]
    row = lax.broadcasted_iota(jnp.int32, x.shape, 0)
    s = 1
    while s < n:
        if reverse:
            x = x + jnp.where(row < n - s, pltpu.roll(x, n - s, axis=0), 0.0)
        else:
            x = x + jnp.where(row >= s, pltpu.roll(x, s, axis=0), 0.0)
        s *= 2
    return x


def _scan_chunk(q_raw, z, val, la, l1m, oml, st_ref, sc_ref, o_ref, q_s, k_s, b_s, v_s, reverse):
    n = q_raw.shape[0]
    q = q_raw * _sigmoid(q_raw)
    e = jnp.exp(-jnp.abs(z))
    log_sig = jnp.minimum(z, 0.0) - jnp.log1p(e)
    c1 = l1m + log_sig
    logf = jnp.maximum(la, c1) + jnp.log1p(jnp.exp(-jnp.abs(la - c1)))
    k = oml * (jnp.where(z >= 0.0, e, 1.0) / (1.0 + e))
    b = _cumsum_rows(logf, reverse)
    q_s[...] = q
    k_s[...] = k
    b_s[...] = b
    v_s[...] = val
    bound = b_s[0:1, :] if reverse else b_s[n - 1:n, :]

    st = st_ref[...]
    v16 = val.astype(BF16)
    o = _dot_nt((q * jnp.exp(b)).astype(BF16), st.astype(BF16))
    kd = (k * jnp.exp(bound - b)).astype(BF16)
    st_ref[...] = st * jnp.exp(bound) + jnp.dot(val.T.astype(BF16), kd, preferred_element_type=F32)

    lane = lax.broadcasted_iota(jnp.int32, (SUBLANES, LANES), 1)
    m = n // 2
    while m >= SCAN_DIAG:
        groups = n // (2 * m)
        t_off = 0 if reverse else m
        s_off = m if reverse else 0
        r_off = m - 1 if reverse else m
        bref = jnp.concatenate(
            [jnp.broadcast_to(b_s[g * 2 * m + r_off:g * 2 * m + r_off + 1, :], (2 * m, LANES))
             for g in range(groups)], axis=0)
        ew = jnp.exp(-jnp.abs(b_s[...] - bref))
        qe = q_s[...] * ew
        ke = (k_s[...] * ew).astype(BF16)
        qt = jnp.concatenate([qe[g * 2 * m + t_off:g * 2 * m + t_off + m] for g in range(groups)],
                             axis=0).astype(BF16)
        g_all = _dot_nt(qt, ke)
        for g in range(groups):
            t0 = g * 2 * m + t_off
            s0 = g * 2 * m + s_off
            tile0 = (s0 // LANES) * LANES
            if m >= LANES:
                sc_ref[t0:t0 + m, s0:s0 + m] = g_all[g * m:(g + 1) * m, s0:s0 + m]
            else:
                keep = (lane >= s0 - tile0) & (lane < s0 - tile0 + m)
                for i in range(m // SUBLANES):
                    rows = slice(t0 + i * SUBLANES, t0 + (i + 1) * SUBLANES)
                    new = g_all[g * m + i * SUBLANES:g * m + (i + 1) * SUBLANES, tile0:tile0 + LANES]
                    sc_ref[rows, tile0:tile0 + LANES] = jnp.where(
                        keep, new, sc_ref[rows, tile0:tile0 + LANES])
        m //= 2

    o_ref[...] = o + jnp.dot(sc_ref[...].astype(BF16), v16, preferred_element_type=F32)

    trow = lax.broadcasted_iota(jnp.int32, (SCAN_DIAG, LANES), 0)

    def diag_block(i, carry):
        t0 = pl.multiple_of(i * SCAN_DIAG, SCAN_DIAG)
        qb = q_s[pl.ds(t0, SCAN_DIAG), :]
        bb = b_s[pl.ds(t0, SCAN_DIAG), :]
        acc = jnp.zeros((SCAN_DIAG, LANES), F32)
        for s in range(SCAN_DIAG):
            ks = k_s[pl.ds(t0 + s, 1), :]
            bs = b_s[pl.ds(t0 + s, 1), :]
            vs = v_s[pl.ds(t0 + s, 1), :]
            w = qb * ks * jnp.exp(bb - bs)
            w = jnp.where((trow <= s) if reverse else (trow >= s), w, 0.0)
            acc = acc + jnp.sum(w, axis=-1, keepdims=True) * vs
        o_ref[pl.ds(t0, SCAN_DIAG), :] += acc
        return carry

    lax.fori_loop(0, n // SCAN_DIAG, diag_block, 0)


def _scan_kernel(*refs, n_chunks, hb, has_init, want_final):
    lb_ref, qf_ref, vf_ref, zf_ref, qb_ref, vb_ref, zb_ref = refs[:7]
    pos = 7
    s0_ref = None
    if has_init:
        s0_ref = refs[pos]
        pos += 1
    of_ref, ob_ref = refs[pos:pos + 2]
    pos += 2
    sfin_ref = None
    if want_final:
        sfin_ref = refs[pos]
        pos += 1
    st_ref, sc_ref, q_s, k_s, b_s, v_s = refs[pos:]
    c = pl.program_id(2)

    @pl.when(c == 0)
    def _():
        sc_ref[...] = jnp.zeros(sc_ref.shape, F32)
        for d in range(2):
            for h in range(hb):
                if has_init:
                    st_ref[d, h] = s0_ref[d, h].T
                else:
                    st_ref[d, h] = jnp.zeros((HEAD_DIM, HEAD_DIM), F32)

    for h in range(hb):
        lanes = slice(h * HEAD_DIM, (h + 1) * HEAD_DIM)
        for d, (q_ref, v_ref, z_ref, o_ref) in enumerate(
                ((qf_ref, vf_ref, zf_ref, of_ref), (qb_ref, vb_ref, zb_ref, ob_ref))):
            _scan_chunk(q_ref[:, lanes], z_ref[:, lanes], v_ref[:, lanes],
                        lb_ref[3 * d:3 * d + 1, lanes], lb_ref[3 * d + 1:3 * d + 2, lanes],
                        lb_ref[3 * d + 2:3 * d + 3, lanes],
                        st_ref.at[d, h], sc_ref.at[d], o_ref.at[:, lanes],
                        q_s, k_s, b_s, v_s, reverse=(d == 1))

    if want_final:
        @pl.when(c == n_chunks - 1)
        def _():
            for d in range(2):
                for h in range(hb):
                    sfin_ref[d, h] = st_ref[d, h].T


def _hgrn_scan(p, lb_terms, state, layer, want_final):
    bsz, l, _ = p.shape
    d_hg = lb_terms.shape[1]
    heads = d_hg // HEAD_DIM
    hb = SCAN_HEADS_PER_STEP
    w = hb * HEAD_DIM
    n_chunks = l // SCAN_CHUNK
    seg = d_hg // w
    has_init = state is not None

    def fwd(col):
        return pl.BlockSpec((None, SCAN_CHUNK, w), lambda b, g, c: (b, c, col * seg + g))

    def bwd(col):
        return pl.BlockSpec((None, SCAN_CHUNK, w), lambda b, g, c: (b, n_chunks - 1 - c, col * seg + g))

    in_specs = [pl.BlockSpec((6, w), lambda b, g, c: (0, g)),
                fwd(0), fwd(3), fwd(1), bwd(0), bwd(3), bwd(2)]
    args = [lb_terms, p, p, p, p, p, p]
    if has_init:
        in_specs.append(pl.BlockSpec((None, None, 2, hb, HEAD_DIM, HEAD_DIM),
                                     lambda b, g, c: (b, layer, 0, g, 0, 0)))
        args.append(state)
    out_specs = [pl.BlockSpec((None, SCAN_CHUNK, w), lambda b, g, c: (b, c, g)),
                 pl.BlockSpec((None, SCAN_CHUNK, w), lambda b, g, c: (b, n_chunks - 1 - c, g))]
    out_shape = [jax.ShapeDtypeStruct((bsz, l, d_hg), F32)] * 2
    if want_final:
        out_specs.append(pl.BlockSpec((None, 2, hb, HEAD_DIM, HEAD_DIM), lambda b, g, c: (b, 0, g, 0, 0)))
        out_shape.append(jax.ShapeDtypeStruct((bsz, 2, heads, HEAD_DIM, HEAD_DIM), F32))
    scratch = [pltpu.VMEM((2, hb, HEAD_DIM, HEAD_DIM), F32),
               pltpu.VMEM((2, SCAN_CHUNK, SCAN_CHUNK), F32)] + [pltpu.VMEM((SCAN_CHUNK, HEAD_DIM), F32)] * 4
    return pl.pallas_call(
        functools.partial(_scan_kernel, n_chunks=n_chunks, hb=hb, has_init=has_init, want_final=want_final),
        grid=(bsz, heads // hb, n_chunks),
        in_specs=in_specs,
        out_specs=out_specs,
        out_shape=out_shape,
        scratch_shapes=scratch,
        compiler_params=_params(3),
        name="hgrn_scan",
    )(*args)


def _hg_post_kernel(of_ref, ob_ref, g_ref, ng_ref, o_ref):
    heads = of_ref.shape[-1] // HEAD_DIM
    for h in range(heads):
        lanes = slice(h * HEAD_DIM, (h + 1) * HEAD_DIM)
        o = of_ref[:, lanes] + ob_ref[:, lanes]
        o = o * lax.rsqrt(jnp.mean(o * o, axis=-1, keepdims=True) + EPS) * ng_ref[...]
        o_ref[:, lanes] = (o * _silu(g_ref[:, lanes])).astype(o_ref.dtype)


def _hg_post(o_f, o_b, p, norm_g, tl=256):
    b, l, d_hg = o_f.shape
    blk = pl.BlockSpec((None, tl, d_hg), lambda i, j: (i, j, 0))
    return pl.pallas_call(
        _hg_post_kernel,
        grid=(b, l // tl),
        in_specs=[blk, blk,
                  pl.BlockSpec((None, tl, d_hg), lambda i, j: (i, j, 4)),
                  pl.BlockSpec((1, HEAD_DIM), lambda i, j: (0, 0))],
        out_specs=blk,
        out_shape=jax.ShapeDtypeStruct((b, l, d_hg), BF16),
        compiler_params=_params(2),
        name="hg_post",
    )(o_f, o_b, p, norm_g.reshape(1, HEAD_DIM))


def _conv_seq_kernel(a_ref, b_ref, w_ref, bias_ref, y_ref, pad_ref, *, seg):
    tokens, ch = a_ref.shape
    halo = 2 * SUBLANES
    stride = seg + 2 * halo
    sub = min(seg, CONV_ROWS)
    u = a_ref[...] * _sigmoid(b_ref[...])
    for r in range(tokens // seg):
        base = r * stride
        pad_ref[base:base + halo, :] = jnp.zeros((halo, ch), F32)
        pad_ref[base + halo:base + halo + seg, :] = u[r * seg:(r + 1) * seg]
        pad_ref[base + halo + seg:base + stride, :] = jnp.zeros((halo, ch), F32)
    for r in range(tokens // seg):
        for t0 in range(0, seg, sub):
            base = r * stride + halo - CONV_HALF + t0
            for c0 in range(0, ch, LANES):
                acc = jnp.zeros((sub, LANES), F32) + bias_ref[:, c0:c0 + LANES]
                for j in range(CONV_TAPS):
                    acc = acc + pad_ref[base + j:base + j + sub, c0:c0 + LANES] * w_ref[j:j + 1, c0:c0 + LANES]
                y_ref[r * seg + t0:r * seg + t0 + sub, c0:c0 + LANES] = acc


def _conv_seq(p, w, bias, seg, tokens=256, tc=512):
    bsz, l, _ = p.shape
    ch = w.shape[1]
    tc = min(tc, ch)
    seg_blocks = ch // tc
    halo = 2 * SUBLANES
    return pl.pallas_call(
        functools.partial(_conv_seq_kernel, seg=seg),
        grid=(bsz, l // tokens, ch // tc),
        in_specs=[pl.BlockSpec((None, tokens, tc), lambda b, t, c: (b, t, 5 * seg_blocks + c)),
                  pl.BlockSpec((None, tokens, tc), lambda b, t, c: (b, t, 6 * seg_blocks + c)),
                  pl.BlockSpec((CONV_TAPS, tc), lambda b, t, c: (0, c)),
                  pl.BlockSpec((1, tc), lambda b, t, c: (0, c))],
        out_specs=pl.BlockSpec((None, tokens, tc), lambda b, t, c: (b, t, c)),
        out_shape=jax.ShapeDtypeStruct((bsz, l, ch), F32),
        scratch_shapes=[pltpu.VMEM(((tokens // seg) * (seg + 2 * halo), tc), F32)],
        compiler_params=_params(3),
        name="conv_seq",
    )(p, p, w, bias.reshape(1, ch))


def _conv_col_kernel(a_ref, b_ref, w_ref, bias_ref, y_ref, pad_ref, *, width):
    tokens, ch = a_ref.shape
    halo = CONV_HALF * width
    pad_ref[0:halo, :] = jnp.zeros((halo, ch), F32)
    pad_ref[halo:halo + tokens, :] = a_ref[...] * _sigmoid(b_ref[...])
    pad_ref[halo + tokens:halo + tokens + halo, :] = jnp.zeros((halo, ch), F32)

    def row(r, carry):
        t0 = pl.multiple_of(r * width, width)
        for c0 in range(0, ch, LANES):
            acc = jnp.zeros((width, LANES), F32) + bias_ref[:, c0:c0 + LANES]
            for j in range(CONV_TAPS):
                acc = acc + pad_ref[pl.ds(t0 + j * width, width), c0:c0 + LANES] * w_ref[j:j + 1, c0:c0 + LANES]
            y_ref[pl.ds(t0, width), c0:c0 + LANES] = acc
        return carry

    lax.fori_loop(0, tokens // width, row, 0)


def _conv_col(p, w, bias, width, tc=256):
    bsz, l, _ = p.shape
    ch = w.shape[1]
    seg_blocks = ch // tc
    return pl.pallas_call(
        functools.partial(_conv_col_kernel, width=width),
        grid=(bsz, ch // tc),
        in_specs=[pl.BlockSpec((None, l, tc), lambda b, c: (b, 0, 5 * seg_blocks + c)),
                  pl.BlockSpec((None, l, tc), lambda b, c: (b, 0, 6 * seg_blocks + c)),
                  pl.BlockSpec((CONV_TAPS, tc), lambda b, c: (0, c)),
                  pl.BlockSpec((1, tc), lambda b, c: (0, c))],
        out_specs=pl.BlockSpec((None, l, tc), lambda b, c: (b, 0, c)),
        out_shape=jax.ShapeDtypeStruct((bsz, l, ch), F32),
        scratch_shapes=[pltpu.VMEM((l + 2 * CONV_HALF * width, tc), F32)],
        compiler_params=_params(2),
        name="conv_col",
    )(p, p, w, bias.reshape(1, ch))


def _cv_post_kernel(y_ref, g_ref, lg_ref, lb_ref, o_ref):
    y = y_ref[...]
    mu = jnp.mean(y, axis=-1, keepdims=True)
    yc = y - mu
    yn = yc * lax.rsqrt(jnp.mean(yc * yc, axis=-1, keepdims=True) + EPS) * lg_ref[...] + lb_ref[...]
    o_ref[...] = (_silu(yn) * _silu(g_ref[...])).astype(o_ref.dtype)


def _cv_post(y, p, ln_g, ln_b, tl=256):
    b, l, ch = y.shape
    blk = pl.BlockSpec((None, tl, ch), lambda i, j: (i, j, 0))
    vec = pl.BlockSpec((1, ch), lambda i, j: (0, 0))
    return pl.pallas_call(
        _cv_post_kernel,
        grid=(b, l // tl),
        in_specs=[blk, pl.BlockSpec((None, tl, ch), lambda i, j: (i, j, 7)), vec, vec],
        out_specs=blk,
        out_shape=jax.ShapeDtypeStruct((b, l, ch), BF16),
        compiler_params=_params(2),
        name="cv_post",
    )(y, p, ln_g.reshape(1, ch), ln_b.reshape(1, ch))


def _layer(x, mod, row_of_batch, lb_terms, state, layer, grid_mode, want_final, norm_g, w_in,
           hg_norm_g, w_dw, b_dw, ln_g, ln_b, w_hproj, w_cproj, w_out):
    bsz, l, d = x.shape
    d_hg = w_hproj.shape[1]
    h = _prenorm(x, norm_g, mod, row_of_batch)
    p = _in_proj(h.reshape(bsz * l, d), w_in, layer)
    p3 = p.reshape(bsz, l, -1)
    scan = _hgrn_scan(p3, lb_terms, state, layer, want_final)
    o_hg = _hg_post(scan[0], scan[1], p3, hg_norm_g)
    if grid_mode == "col":
        y = _conv_col(p3, w_dw, b_dw, LATENT_GRID_W)
    else:
        y = _conv_seq(p3, w_dw, b_dw, LATENT_GRID_W if grid_mode == "row" else l)
    o_cv = _cv_post(y, p3, ln_g, ln_b)
    merged = _merge_proj(o_hg.reshape(bsz * l, d_hg), w_hproj, o_cv.reshape(bsz * l, -1), w_cproj,
                         layer, p, 4 * d)
    x_new = _out_proj(merged, w_out, layer, x.reshape(bsz * l, d), mod, l, row_of_batch)
    return x_new.reshape(bsz, l, d), (scan[2] if want_final else None)


def kernel(x_prompt, x_sample, state_hgrn, c, c_ctx, mod_w, mod_b, norm_g, w_in, hg_lb_logits, hg_norm_g,
           cv_dw_w, cv_dw_b, cv_ln_g, cv_ln_b, w_hproj, w_cproj, w_out, final_g):
    depth = mod_w.shape[0]
    d = x_prompt.shape[-1]
    dec_batch = c.shape[0]
    rows = 1 + dec_batch
    pad = (-rows) % SUBLANES
    cond = jnp.concatenate([c_ctx.reshape(1, d), c, jnp.zeros((pad, d), F32)], axis=0).astype(F32)
    mod = _modulation(cond, mod_w, mod_b)
    mod = mod.reshape(depth, rows + pad, 1, 3 * d)
    lb = _lower_bound_terms(hg_lb_logits)
    d_hg = hg_lb_logits.shape[-1]
    lb = lb.reshape(3, depth, 2, d_hg)

    w_in16 = w_in.astype(BF16)
    w_h16 = w_hproj.astype(BF16)
    w_c16 = w_cproj.astype(BF16)
    w_o16 = w_out.astype(BF16)

    xp, xs = x_prompt, x_sample
    states = []
    for l in range(depth):
        lb_l = jnp.transpose(lb[:, l], (1, 0, 2)).reshape(6, d_hg)
        shared = (norm_g[l], w_in16, hg_norm_g[l], cv_dw_w[l], cv_dw_b[l], cv_ln_g[l], cv_ln_b[l],
                  w_h16, w_c16, w_o16)
        xp, s_fin = _layer(xp, mod[l], lambda b: 0, lb_l, None, l, None, True, *shared)
        states.append(s_fin)
        xs, _ = _layer(xs, mod[l], lambda b: 1 + b, lb_l, state_hgrn, l,
                       "row" if l % 2 == 0 else "col", False, *shared)
    y_prompt = _final_norm(xp, final_g)
    y_sample = _final_norm(xs, final_g)
    return y_prompt, y_sample, jnp.stack(states, axis=1)
```

```python
import functools

import jax
import jax.numpy as jnp
from jax import lax
from jax.experimental import pallas as pl
from jax.experimental.pallas import tpu as pltpu

F32 = jnp.float32
BF16 = jnp.bfloat16

EPS = 1e-6
HEAD_DIM = 128
CONV_TAPS = 31
CONV_HALF = CONV_TAPS // 2
CONV_ROWS = 64
LATENT_GRID_W = 64
SCAN_CHUNK = 256
SCAN_DIAG = 8
SCAN_HEADS_PER_STEP = 2
LANES = 128
SUBLANES = 8
VMEM_LIMIT_BYTES = 56 * 1024 * 1024


def _params(n_axes):
    return pltpu.CompilerParams(dimension_semantics=("arbitrary",) * n_axes,
                                vmem_limit_bytes=VMEM_LIMIT_BYTES)


def _sigmoid(x):
    return 0.5 * jnp.tanh(0.5 * x) + 0.5


def _silu(x):
    return x * _sigmoid(x)


def _lb_kernel(logit_ref, out_ref):
    depth = logit_ref.shape[0]
    x = logit_ref[...]
    mx = jnp.max(x, axis=0, keepdims=True)
    ex = jnp.exp(x - mx)
    p = ex / jnp.sum(ex, axis=0, keepdims=True)
    p0 = p[0:1]
    cum = p0
    for l in range(depth):
        if l > 0:
            cum = cum + p[l:l + 1]
        lb = cum - p0
        out_ref[0, l:l + 1, :] = jnp.log(lb)
        out_ref[1, l:l + 1, :] = 1.0 - lb


def _lower_bound_terms(lb_logits):
    depth = lb_logits.shape[0]
    flat = lb_logits.reshape(depth, -1).astype(F32)
    return pl.pallas_call(
        _lb_kernel,
        out_shape=jax.ShapeDtypeStruct((2, depth, flat.shape[1]), F32),
        name="lb_terms",
    )(flat)


def _mod_kernel(c_ref, w_ref, b_ref, o_ref):
    a = _silu(c_ref[...]).astype(BF16)
    w = w_ref[...].astype(BF16)
    o_ref[...] = jnp.dot(a, w, preferred_element_type=F32) + b_ref[...]


def _modulation(cond, mod_w, mod_b, tn=512):
    depth, d, n = mod_w.shape
    r = cond.shape[0]
    return pl.pallas_call(
        _mod_kernel,
        grid=(depth, n // tn),
        in_specs=[pl.BlockSpec((r, d), lambda l, j: (0, 0)),
                  pl.BlockSpec((None, d, tn), lambda l, j: (l, 0, j)),
                  pl.BlockSpec((None, 1, tn), lambda l, j: (l, 0, j))],
        out_specs=pl.BlockSpec((None, r, tn), lambda l, j: (l, 0, j)),
        out_shape=jax.ShapeDtypeStruct((depth, r, n), F32),
        compiler_params=_params(2),
        name="adaln_mod",
    )(cond, mod_w, mod_b.reshape(depth, 1, n))


def _prenorm_kernel(x_ref, g_ref, mod_ref, h_ref):
    d = x_ref.shape[-1]
    x = x_ref[...]
    y = x * lax.rsqrt(jnp.mean(x * x, axis=-1, keepdims=True) + EPS) * g_ref[...]
    shift = mod_ref[:, 0:d]
    scale = mod_ref[:, d:2 * d]
    h_ref[...] = (y * (1.0 + scale) + shift).astype(h_ref.dtype)


def _prenorm(x, g, mod, row_of_batch, tl=256):
    b, l, d = x.shape
    return pl.pallas_call(
        _prenorm_kernel,
        grid=(b, l // tl),
        in_specs=[pl.BlockSpec((None, tl, d), lambda i, j: (i, j, 0)),
                  pl.BlockSpec((1, d), lambda i, j: (0, 0)),
                  pl.BlockSpec((None, 1, 3 * d), lambda i, j: (row_of_batch(i), 0, 0))],
        out_specs=pl.BlockSpec((None, tl, d), lambda i, j: (i, j, 0)),
        out_shape=jax.ShapeDtypeStruct((b, l, d), BF16),
        compiler_params=_params(2),
        name="prenorm",
    )(x, g.reshape(1, d), mod)


def _final_norm_kernel(x_ref, g_ref, o_ref):
    x = x_ref[...]
    o_ref[...] = x * lax.rsqrt(jnp.mean(x * x, axis=-1, keepdims=True) + EPS) * g_ref[...]


def _final_norm(x, g, tl=256):
    b, l, d = x.shape
    return pl.pallas_call(
        _final_norm_kernel,
        grid=(b, l // tl),
        in_specs=[pl.BlockSpec((None, tl, d), lambda i, j: (i, j, 0)),
                  pl.BlockSpec((1, d), lambda i, j: (0, 0))],
        out_specs=pl.BlockSpec((None, tl, d), lambda i, j: (i, j, 0)),
        out_shape=jax.ShapeDtypeStruct((b, l, d), F32),
        compiler_params=_params(2),
        name="final_norm",
    )(x, g.reshape(1, d))


def _matmul_kernel(a_ref, w_ref, o_ref):
    o_ref[...] = jnp.dot(a_ref[...], w_ref[...], preferred_element_type=F32).astype(o_ref.dtype)


def _in_proj(h, w, layer, tm=1024, tn=1024):
    m, k = h.shape
    n = w.shape[2]
    tm = min(tm, m)
    return pl.pallas_call(
        _matmul_kernel,
        grid=(m // tm, n // tn),
        in_specs=[pl.BlockSpec((tm, k), lambda i, j: (i, 0)),
                  pl.BlockSpec((None, k, tn), lambda i, j: (layer, 0, j))],
        out_specs=pl.BlockSpec((tm, tn), lambda i, j: (i, j)),
        out_shape=jax.ShapeDtypeStruct((m, n), F32),
        compiler_params=_params(2),
        name="in_proj",
    )(h, w)


def _merge_kernel(oh_ref, wh_ref, oc_ref, wc_ref, mh_ref, mc_ref, o_ref):
    yh = jnp.dot(oh_ref[...], wh_ref[...], preferred_element_type=F32)
    yc = jnp.dot(oc_ref[...], wc_ref[...], preferred_element_type=F32)
    o_ref[...] = (_sigmoid(mh_ref[...]) * yh + _sigmoid(mc_ref[...]) * yc).astype(o_ref.dtype)


def _merge_proj(o_hg, w_h, o_cv, w_c, layer, p, gate_col0, tm=1024, tn=512):
    m, kh = o_hg.shape
    kc = o_cv.shape[1]
    n = w_h.shape[2]
    tm = min(tm, m)
    gh = gate_col0 // tn
    gc = (gate_col0 + n) // tn
    return pl.pallas_call(
        _merge_kernel,
        grid=(m // tm, n // tn),
        in_specs=[pl.BlockSpec((tm, kh), lambda i, j: (i, 0)),
                  pl.BlockSpec((None, kh, tn), lambda i, j: (layer, 0, j)),
                  pl.BlockSpec((tm, kc), lambda i, j: (i, 0)),
                  pl.BlockSpec((None, kc, tn), lambda i, j: (layer, 0, j)),
                  pl.BlockSpec((tm, tn), lambda i, j: (i, gh + j)),
                  pl.BlockSpec((tm, tn), lambda i, j: (i, gc + j))],
        out_specs=pl.BlockSpec((tm, tn), lambda i, j: (i, j)),
        out_shape=jax.ShapeDtypeStruct((m, n), BF16),
        compiler_params=_params(2),
        name="merge_proj",
    )(o_hg, w_h, o_cv, w_c, p, p)


def _out_kernel(a_ref, w_ref, x_ref, gate_ref, o_ref):
    y = jnp.dot(a_ref[...], w_ref[...], preferred_element_type=F32)
    o_ref[...] = x_ref[...] + gate_ref[...] * y


def _out_proj(merged, w, layer, x, mod, rows_per_batch, row_of_batch, tm=1024, tn=512):
    m, k = merged.shape
    d = w.shape[2]
    tm = min(tm, m, rows_per_batch)
    blocks_per_batch = rows_per_batch // tm
    gate0 = 2 * d // tn
    return pl.pallas_call(
        _out_kernel,
        grid=(m // tm, d // tn),
        in_specs=[pl.BlockSpec((tm, k), lambda i, j: (i, 0)),
                  pl.BlockSpec((None, k, tn), lambda i, j: (layer, 0, j)),
                  pl.BlockSpec((tm, tn), lambda i, j: (i, j)),
                  pl.BlockSpec((None, 1, tn),
                               lambda i, j: (row_of_batch(i // blocks_per_batch), 0, gate0 + j))],
        out_specs=pl.BlockSpec((tm, tn), lambda i, j: (i, j)),
        out_shape=jax.ShapeDtypeStruct((m, d), F32),
        compiler_params=_params(2),
        name="out_proj",
    )(merged, w, x, mod)


def _dot_nt(a, b):
    return lax.dot_general(a, b, (((1,), (1,)), ((), ())), preferred_element_type=F32)


def _cumsum_rows(x, reverse):
    n = x.shape[0]
    row = lax.broadcasted_iota(jnp.int32, x.shape, 0)
    s = 1
    while s < n:
        if reverse:
            x = x + jnp.where(row < n - s, pltpu.roll(x, n - s, axis=0), 0.0)
        else:
            x = x + jnp.where(row >= s, pltpu.roll(x, s, axis=0), 0.0)
        s *= 2
    return x


def _scan_chunk(q_raw, z, val, la, oml, st_ref, sc_ref, o_ref, q_s, k_s, b_s, v_s, reverse):
    n = q_raw.shape[0]
    q = q_raw * _sigmoid(q_raw)
    logf = (jnp.maximum(la, z) - jnp.maximum(z, 0.0)
            + jnp.log(1.0 + jnp.exp(-jnp.abs(la - z))) - jnp.log(1.0 + jnp.exp(-jnp.abs(z))))
    k = oml * _sigmoid(-z)
    b = _cumsum_rows(logf, reverse)
    q_s[...] = q
    k_s[...] = k
    b_s[...] = b
    v_s[...] = val
    bound = b_s[0:1, :] if reverse else b_s[n - 1:n, :]

    st = st_ref[...]
    v16 = val.astype(BF16)
    o = _dot_nt((q * jnp.exp(b)).astype(BF16), st.astype(BF16))
    kd = (k * jnp.exp(bound - b)).astype(BF16)
    st_ref[...] = st * jnp.exp(bound) + jnp.dot(val.T.astype(BF16), kd, preferred_element_type=F32)

    lane = lax.broadcasted_iota(jnp.int32, (SUBLANES, LANES), 1)
    m = n // 2
    while m >= SCAN_DIAG:
        groups = n // (2 * m)
        t_off = 0 if reverse else m
        s_off = m if reverse else 0
        r_off = m - 1 if reverse else m
        bref = jnp.concatenate(
            [jnp.broadcast_to(b_s[g * 2 * m + r_off:g * 2 * m + r_off + 1, :], (2 * m, LANES))
             for g in range(groups)], axis=0)
        ew = jnp.exp(-jnp.abs(b_s[...] - bref))
        qe = q_s[...] * ew
        ke = (k_s[...] * ew).astype(BF16)
        qt = jnp.concatenate([qe[g * 2 * m + t_off:g * 2 * m + t_off + m] for g in range(groups)],
                             axis=0).astype(BF16)
        g_all = _dot_nt(qt, ke)
        for g in range(groups):
            t0 = g * 2 * m + t_off
            s0 = g * 2 * m + s_off
            tile0 = (s0 // LANES) * LANES
            if m >= LANES:
                sc_ref[t0:t0 + m, s0:s0 + m] = g_all[g * m:(g + 1) * m, s0:s0 + m]
            else:
                keep = (lane >= s0 - tile0) & (lane < s0 - tile0 + m)
                for i in range(m // SUBLANES):
                    rows = slice(t0 + i * SUBLANES, t0 + (i + 1) * SUBLANES)
                    new = g_all[g * m + i * SUBLANES:g * m + (i + 1) * SUBLANES, tile0:tile0 + LANES]
                    sc_ref[rows, tile0:tile0 + LANES] = jnp.where(
                        keep, new, sc_ref[rows, tile0:tile0 + LANES])
        m //= 2

    o = o + jnp.dot(sc_ref[...].astype(BF16), v16, preferred_element_type=F32)

    trow = lax.broadcasted_iota(jnp.int32, (SCAN_DIAG, LANES), 0)
    diag = []
    for t0 in range(0, n, SCAN_DIAG):
        qb = q_s[t0:t0 + SCAN_DIAG, :]
        bb = b_s[t0:t0 + SCAN_DIAG, :]
        acc = None
        for s in range(SCAN_DIAG):
            ks = k_s[t0 + s:t0 + s + 1, :]
            bs = b_s[t0 + s:t0 + s + 1, :]
            vs = v_s[t0 + s:t0 + s + 1, :]
            w = qb * ks * jnp.exp(bb - bs)
            if 0 < s < SCAN_DIAG - 1 or (s == 0 and reverse) or (s == SCAN_DIAG - 1 and not reverse):
                w = jnp.where((trow <= s) if reverse else (trow >= s), w, 0.0)
            term = jnp.sum(w, axis=-1, keepdims=True) * vs
            acc = term if acc is None else acc + term
        diag.append(acc)
    o_ref[...] = o + jnp.concatenate(diag, axis=0)


def _scan_kernel(*refs, n_chunks, hb, has_init, want_final):
    lb_ref, qf_ref, vf_ref, zf_ref, qb_ref, vb_ref, zb_ref = refs[:7]
    pos = 7
    s0_ref = None
    if has_init:
        s0_ref = refs[pos]
        pos += 1
    of_ref, ob_ref = refs[pos:pos + 2]
    pos += 2
    sfin_ref = None
    if want_final:
        sfin_ref = refs[pos]
        pos += 1
    st_ref, sc_ref, tmp_ref = refs[pos:]
    c = pl.program_id(2)

    @pl.when(c == 0)
    def _():
        sc_ref[...] = jnp.zeros(sc_ref.shape, F32)
        for d in range(2):
            for h in range(hb):
                if has_init:
                    st_ref[d, h] = s0_ref[d, h].T
                else:
                    st_ref[d, h] = jnp.zeros((HEAD_DIM, HEAD_DIM), F32)

    for h in range(hb):
        lanes = slice(h * HEAD_DIM, (h + 1) * HEAD_DIM)
        for d, (q_ref, v_ref, z_ref, o_ref) in enumerate(
                ((qf_ref, vf_ref, zf_ref, of_ref), (qb_ref, vb_ref, zb_ref, ob_ref))):
            _scan_chunk(q_ref[:, lanes], z_ref[:, lanes], v_ref[:, lanes],
                        lb_ref[2 * d:2 * d + 1, lanes], lb_ref[2 * d + 1:2 * d + 2, lanes],
                        st_ref.at[d, h], sc_ref.at[d, h], o_ref.at[:, lanes],
                        *(tmp_ref.at[d, h, i] for i in range(4)), reverse=(d == 1))

    if want_final:
        @pl.when(c == n_chunks - 1)
        def _():
            for d in range(2):
                for h in range(hb):
                    sfin_ref[d, h] = st_ref[d, h].T


def _hgrn_scan(p, lb_terms, state, layer, want_final):
    bsz, l, _ = p.shape
    d_hg = lb_terms.shape[1]
    heads = d_hg // HEAD_DIM
    hb = SCAN_HEADS_PER_STEP
    w = hb * HEAD_DIM
    n_chunks = l // SCAN_CHUNK
    seg = d_hg // w
    has_init = state is not None

    def fwd(col):
        return pl.BlockSpec((None, SCAN_CHUNK, w), lambda b, g, c: (b, c, col * seg + g))

    def bwd(col):
        return pl.BlockSpec((None, SCAN_CHUNK, w), lambda b, g, c: (b, n_chunks - 1 - c, col * seg + g))

    in_specs = [pl.BlockSpec((4, w), lambda b, g, c: (0, g)),
                fwd(0), fwd(3), fwd(1), bwd(0), bwd(3), bwd(2)]
    args = [lb_terms, p, p, p, p, p, p]
    if has_init:
        in_specs.append(pl.BlockSpec((None, None, 2, hb, HEAD_DIM, HEAD_DIM),
                                     lambda b, g, c: (b, layer, 0, g, 0, 0)))
        args.append(state)
    out_specs = [pl.BlockSpec((None, SCAN_CHUNK, w), lambda b, g, c: (b, c, g)),
                 pl.BlockSpec((None, SCAN_CHUNK, w), lambda b, g, c: (b, n_chunks - 1 - c, g))]
    out_shape = [jax.ShapeDtypeStruct((bsz, l, d_hg), F32)] * 2
    if want_final:
        out_specs.append(pl.BlockSpec((None, 2, hb, HEAD_DIM, HEAD_DIM), lambda b, g, c: (b, 0, g, 0, 0)))
        out_shape.append(jax.ShapeDtypeStruct((bsz, 2, heads, HEAD_DIM, HEAD_DIM), F32))
    scratch = [pltpu.VMEM((2, hb, HEAD_DIM, HEAD_DIM), F32),
               pltpu.VMEM((2, hb, SCAN_CHUNK, SCAN_CHUNK), F32),
               pltpu.VMEM((2, hb, 4, SCAN_CHUNK, HEAD_DIM), F32)]
    return pl.pallas_call(
        functools.partial(_scan_kernel, n_chunks=n_chunks, hb=hb, has_init=has_init, want_final=want_final),
        grid=(bsz, heads // hb, n_chunks),
        in_specs=in_specs,
        out_specs=out_specs,
        out_shape=out_shape,
        scratch_shapes=scratch,
        compiler_params=_params(3),
        name="hgrn_scan",
    )(*args)


def _hg_post_kernel(of_ref, ob_ref, g_ref, ng_ref, o_ref):
    heads = of_ref.shape[-1] // HEAD_DIM
    for h in range(heads):
        lanes = slice(h * HEAD_DIM, (h + 1) * HEAD_DIM)
        o = of_ref[:, lanes] + ob_ref[:, lanes]
        o = o * lax.rsqrt(jnp.mean(o * o, axis=-1, keepdims=True) + EPS) * ng_ref[...]
        o_ref[:, lanes] = (o * _silu(g_ref[:, lanes])).astype(o_ref.dtype)


def _hg_post(o_f, o_b, p, norm_g, tl=256):
    b, l, d_hg = o_f.shape
    blk = pl.BlockSpec((None, tl, d_hg), lambda i, j: (i, j, 0))
    return pl.pallas_call(
        _hg_post_kernel,
        grid=(b, l // tl),
        in_specs=[blk, blk,
                  pl.BlockSpec((None, tl, d_hg), lambda i, j: (i, j, 4)),
                  pl.BlockSpec((1, HEAD_DIM), lambda i, j: (0, 0))],
        out_specs=blk,
        out_shape=jax.ShapeDtypeStruct((b, l, d_hg), BF16),
        compiler_params=_params(2),
        name="hg_post",
    )(o_f, o_b, p, norm_g.reshape(1, HEAD_DIM))


def _conv_seq_kernel(a_ref, b_ref, w_ref, bias_ref, y_ref, pad_ref, *, seg):
    tokens, ch = a_ref.shape
    halo = 2 * SUBLANES
    stride = seg + 2 * halo
    sub = min(seg, CONV_ROWS)
    u = a_ref[...] * _sigmoid(b_ref[...])
    for r in range(tokens // seg):
        base = r * stride
        pad_ref[base:base + halo, :] = jnp.zeros((halo, ch), F32)
        pad_ref[base + halo:base + halo + seg, :] = u[r * seg:(r + 1) * seg]
        pad_ref[base + halo + seg:base + stride, :] = jnp.zeros((halo, ch), F32)
    for r in range(tokens // seg):
        for t0 in range(0, seg, sub):
            base = r * stride + halo - CONV_HALF + t0
            for c0 in range(0, ch, LANES):
                acc = jnp.zeros((sub, LANES), F32) + bias_ref[:, c0:c0 + LANES]
                for j in range(CONV_TAPS):
                    acc = acc + pad_ref[base + j:base + j + sub, c0:c0 + LANES] * w_ref[j:j + 1, c0:c0 + LANES]
                y_ref[r * seg + t0:r * seg + t0 + sub, c0:c0 + LANES] = acc


def _conv_seq(p, w, bias, seg, tokens=256, tc=512):
    bsz, l, _ = p.shape
    ch = w.shape[1]
    tc = min(tc, ch)
    seg_blocks = ch // tc
    halo = 2 * SUBLANES
    return pl.pallas_call(
        functools.partial(_conv_seq_kernel, seg=seg),
        grid=(bsz, l // tokens, ch // tc),
        in_specs=[pl.BlockSpec((None, tokens, tc), lambda b, t, c: (b, t, 5 * seg_blocks + c)),
                  pl.BlockSpec((None, tokens, tc), lambda b, t, c: (b, t, 6 * seg_blocks + c)),
                  pl.BlockSpec((CONV_TAPS, tc), lambda b, t, c: (0, c)),
                  pl.BlockSpec((1, tc), lambda b, t, c: (0, c))],
        out_specs=pl.BlockSpec((None, tokens, tc), lambda b, t, c: (b, t, c)),
        out_shape=jax.ShapeDtypeStruct((bsz, l, ch), F32),
        scratch_shapes=[pltpu.VMEM(((tokens // seg) * (seg + 2 * halo), tc), F32)],
        compiler_params=_params(3),
        name="conv_seq",
    )(p, p, w, bias.reshape(1, ch))


def _conv_col_kernel(a_ref, b_ref, w_ref, bias_ref, y_ref, pad_ref, *, width):
    tokens, ch = a_ref.shape
    halo = CONV_HALF * width
    pad_ref[0:halo, :] = jnp.zeros((halo, ch), F32)
    pad_ref[halo:halo + tokens, :] = a_ref[...] * _sigmoid(b_ref[...])
    pad_ref[halo + tokens:halo + tokens + halo, :] = jnp.zeros((halo, ch), F32)

    def row(r, carry):
        t0 = pl.multiple_of(r * width, width)
        for c0 in range(0, ch, LANES):
            acc = jnp.zeros((width, LANES), F32) + bias_ref[:, c0:c0 + LANES]
            for j in range(CONV_TAPS):
                acc = acc + pad_ref[pl.ds(t0 + j * width, width), c0:c0 + LANES] * w_ref[j:j + 1, c0:c0 + LANES]
            y_ref[pl.ds(t0, width), c0:c0 + LANES] = acc
        return carry

    lax.fori_loop(0, tokens // width, row, 0)


def _conv_col(p, w, bias, width, tc=256):
    bsz, l, _ = p.shape
    ch = w.shape[1]
    seg_blocks = ch // tc
    return pl.pallas_call(
        functools.partial(_conv_col_kernel, width=width),
        grid=(bsz, ch // tc),
        in_specs=[pl.BlockSpec((None, l, tc), lambda b, c: (b, 0, 5 * seg_blocks + c)),
                  pl.BlockSpec((None, l, tc), lambda b, c: (b, 0, 6 * seg_blocks + c)),
                  pl.BlockSpec((CONV_TAPS, tc), lambda b, c: (0, c)),
                  pl.BlockSpec((1, tc), lambda b, c: (0, c))],
        out_specs=pl.BlockSpec((None, l, tc), lambda b, c: (b, 0, c)),
        out_shape=jax.ShapeDtypeStruct((bsz, l, ch), F32),
        scratch_shapes=[pltpu.VMEM((l + 2 * CONV_HALF * width, tc), F32)],
        compiler_params=_params(2),
        name="conv_col",
    )(p, p, w, bias.reshape(1, ch))


def _cv_post_kernel(y_ref, g_ref, lg_ref, lb_ref, o_ref):
    y = y_ref[...]
    mu = jnp.mean(y, axis=-1, keepdims=True)
    yc = y - mu
    yn = yc * lax.rsqrt(jnp.mean(yc * yc, axis=-1, keepdims=True) + EPS) * lg_ref[...] + lb_ref[...]
    o_ref[...] = (_silu(yn) * _silu(g_ref[...])).astype(o_ref.dtype)


def _cv_post(y, p, ln_g, ln_b, tl=256):
    b, l, ch = y.shape
    blk = pl.BlockSpec((None, tl, ch), lambda i, j: (i, j, 0))
    vec = pl.BlockSpec((1, ch), lambda i, j: (0, 0))
    return pl.pallas_call(
        _cv_post_kernel,
        grid=(b, l // tl),
        in_specs=[blk, pl.BlockSpec((None, tl, ch), lambda i, j: (i, j, 7)), vec, vec],
        out_specs=blk,
        out_shape=jax.ShapeDtypeStruct((b, l, ch), BF16),
        compiler_params=_params(2),
        name="cv_post",
    )(y, p, ln_g.reshape(1, ch), ln_b.reshape(1, ch))


def _layer(x, mod, row_of_batch, lb_terms, state, layer, grid_mode, want_final, norm_g, w_in,
           hg_norm_g, w_dw, b_dw, ln_g, ln_b, w_hproj, w_cproj, w_out):
    bsz, l, d = x.shape
    d_hg = w_hproj.shape[1]
    h = _prenorm(x, norm_g, mod, row_of_batch)
    p = _in_proj(h.reshape(bsz * l, d), w_in, layer)
    p3 = p.reshape(bsz, l, -1)
    scan = _hgrn_scan(p3, lb_terms, state, layer, want_final)
    o_hg = _hg_post(scan[0], scan[1], p3, hg_norm_g)
    if grid_mode == "col":
        y = _conv_col(p3, w_dw, b_dw, LATENT_GRID_W)
    else:
        y = _conv_seq(p3, w_dw, b_dw, LATENT_GRID_W if grid_mode == "row" else l)
    o_cv = _cv_post(y, p3, ln_g, ln_b)
    merged = _merge_proj(o_hg.reshape(bsz * l, d_hg), w_hproj, o_cv.reshape(bsz * l, -1), w_cproj,
                         layer, p, 4 * d)
    x_new = _out_proj(merged, w_out, layer, x.reshape(bsz * l, d), mod, l, row_of_batch)
    return x_new.reshape(bsz, l, d), (scan[2] if want_final else None)


def kernel(x_prompt, x_sample, state_hgrn, c, c_ctx, mod_w, mod_b, norm_g, w_in, hg_lb_logits, hg_norm_g,
           cv_dw_w, cv_dw_b, cv_ln_g, cv_ln_b, w_hproj, w_cproj, w_out, final_g):
    depth = mod_w.shape[0]
    d = x_prompt.shape[-1]
    dec_batch = c.shape[0]
    rows = 1 + dec_batch
    pad = (-rows) % SUBLANES
    cond = jnp.concatenate([c_ctx.reshape(1, d), c, jnp.zeros((pad, d), F32)], axis=0).astype(F32)
    mod = _modulation(cond, mod_w, mod_b)
    mod = mod.reshape(depth, rows + pad, 1, 3 * d)
    lb = _lower_bound_terms(hg_lb_logits)
    d_hg = hg_lb_logits.shape[-1]
    lb = lb.reshape(2, depth, 2, d_hg)

    w_in16 = w_in.astype(BF16)
    w_h16 = w_hproj.astype(BF16)
    w_c16 = w_cproj.astype(BF16)
    w_o16 = w_out.astype(BF16)

    xp, xs = x_prompt, x_sample
    states = []
    for l in range(depth):
        lb_l = jnp.transpose(lb[:, l], (1, 0, 2)).reshape(4, d_hg)
        shared = (norm_g[l], w_in16, hg_norm_g[l], cv_dw_w[l], cv_dw_b[l], cv_ln_g[l], cv_ln_b[l],
                  w_h16, w_c16, w_o16)
        xp, s_fin = _layer(xp, mod[l], lambda b: 0, lb_l, None, l, None, True, *shared)
        states.append(s_fin)
        xs, _ = _layer(xs, mod[l], lambda b: 1 + b, lb_l, state_hgrn, l,
                       "row" if l % 2 == 0 else "col", False, *shared)
    y_prompt = _final_norm(xp, final_g)
    y_sample = _final_norm(xs, final_g)
    return y_prompt, y_sample, jnp.stack(states, axis=1)
```

```python
import functools

import jax
import jax.numpy as jnp
from jax import lax
from jax.experimental import pallas as pl
from jax.experimental.pallas import tpu as pltpu

F32 = jnp.float32
BF16 = jnp.bfloat16

EPS = 1e-6
HEAD_DIM = 128
CONV_TAPS = 31
CONV_HALF = CONV_TAPS // 2
CONV_ROWS = 64
LATENT_GRID_W = 64
SCAN_CHUNK = 256
SCAN_DIAG = 8
SCAN_HEADS_PER_STEP = 2
LANES = 128
SUBLANES = 8
VMEM_LIMIT_BYTES = 56 * 1024 * 1024


def _params(n_axes):
    return pltpu.CompilerParams(dimension_semantics=("arbitrary",) * n_axes,
                                vmem_limit_bytes=VMEM_LIMIT_BYTES)


def _sigmoid(x):
    return 0.5 * jnp.tanh(0.5 * x) + 0.5


def _silu(x):
    return x * _sigmoid(x)


def _lb_kernel(logit_ref, out_ref):
    depth = logit_ref.shape[0]
    x = logit_ref[...]
    mx = jnp.max(x, axis=0, keepdims=True)
    ex = jnp.exp(x - mx)
    p = ex / jnp.sum(ex, axis=0, keepdims=True)
    p0 = p[0:1]
    cum = p0
    for l in range(depth):
        if l > 0:
            cum = cum + p[l:l + 1]
        lb = cum - p0
        out_ref[0, l:l + 1, :] = jnp.log(lb)
        out_ref[1, l:l + 1, :] = 1.0 - lb


def _lower_bound_terms(lb_logits):
    depth = lb_logits.shape[0]
    flat = lb_logits.reshape(depth, -1).astype(F32)
    return pl.pallas_call(
        _lb_kernel,
        out_shape=jax.ShapeDtypeStruct((2, depth, flat.shape[1]), F32),
        name="lb_terms",
    )(flat)


def _mod_kernel(c_ref, w_ref, b_ref, o_ref):
    a = _silu(c_ref[...]).astype(BF16)
    w = w_ref[...].astype(BF16)
    o_ref[...] = jnp.dot(a, w, preferred_element_type=F32) + b_ref[...]


def _modulation(cond, mod_w, mod_b, tn=512):
    depth, d, n = mod_w.shape
    r = cond.shape[0]
    return pl.pallas_call(
        _mod_kernel,
        grid=(depth, n // tn),
        in_specs=[pl.BlockSpec((r, d), lambda l, j: (0, 0)),
                  pl.BlockSpec((None, d, tn), lambda l, j: (l, 0, j)),
                  pl.BlockSpec((None, 1, tn), lambda l, j: (l, 0, j))],
        out_specs=pl.BlockSpec((None, r, tn), lambda l, j: (l, 0, j)),
        out_shape=jax.ShapeDtypeStruct((depth, r, n), F32),
        compiler_params=_params(2),
        name="adaln_mod",
    )(cond, mod_w, mod_b.reshape(depth, 1, n))


def _prenorm_kernel(x_ref, g_ref, mod_ref, h_ref):
    d = x_ref.shape[-1]
    x = x_ref[...]
    y = x * lax.rsqrt(jnp.mean(x * x, axis=-1, keepdims=True) + EPS) * g_ref[...]
    shift = mod_ref[:, 0:d]
    scale = mod_ref[:, d:2 * d]
    h_ref[...] = (y * (1.0 + scale) + shift).astype(h_ref.dtype)


def _prenorm(x, g, mod, row_of_batch, tl=256):
    b, l, d = x.shape
    return pl.pallas_call(
        _prenorm_kernel,
        grid=(b, l // tl),
        in_specs=[pl.BlockSpec((None, tl, d), lambda i, j: (i, j, 0)),
                  pl.BlockSpec((1, d), lambda i, j: (0, 0)),
                  pl.BlockSpec((None, 1, 3 * d), lambda i, j: (row_of_batch(i), 0, 0))],
        out_specs=pl.BlockSpec((None, tl, d), lambda i, j: (i, j, 0)),
        out_shape=jax.ShapeDtypeStruct((b, l, d), BF16),
        compiler_params=_params(2),
        name="prenorm",
    )(x, g.reshape(1, d), mod)


def _final_norm_kernel(x_ref, g_ref, o_ref):
    x = x_ref[...]
    o_ref[...] = x * lax.rsqrt(jnp.mean(x * x, axis=-1, keepdims=True) + EPS) * g_ref[...]


def _final_norm(x, g, tl=256):
    b, l, d = x.shape
    return pl.pallas_call(
        _final_norm_kernel,
        grid=(b, l // tl),
        in_specs=[pl.BlockSpec((None, tl, d), lambda i, j: (i, j, 0)),
                  pl.BlockSpec((1, d), lambda i, j: (0, 0))],
        out_specs=pl.BlockSpec((None, tl, d), lambda i, j: (i, j, 0)),
        out_shape=jax.ShapeDtypeStruct((b, l, d), F32),
        compiler_params=_params(2),
        name="final_norm",
    )(x, g.reshape(1, d))


def _matmul_kernel(a_ref, w_ref, o_ref):
    o_ref[...] = jnp.dot(a_ref[...], w_ref[...], preferred_element_type=F32).astype(o_ref.dtype)


def _in_proj(h, w, layer, tm=1024, tn=1024):
    m, k = h.shape
    n = w.shape[2]
    tm = min(tm, m)
    return pl.pallas_call(
        _matmul_kernel,
        grid=(m // tm, n // tn),
        in_specs=[pl.BlockSpec((tm, k), lambda i, j: (i, 0)),
                  pl.BlockSpec((None, k, tn), lambda i, j: (layer, 0, j))],
        out_specs=pl.BlockSpec((tm, tn), lambda i, j: (i, j)),
        out_shape=jax.ShapeDtypeStruct((m, n), F32),
        compiler_params=_params(2),
        name="in_proj",
    )(h, w)


def _merge_kernel(oh_ref, wh_ref, oc_ref, wc_ref, mh_ref, mc_ref, o_ref):
    yh = jnp.dot(oh_ref[...], wh_ref[...], preferred_element_type=F32)
    yc = jnp.dot(oc_ref[...], wc_ref[...], preferred_element_type=F32)
    o_ref[...] = (_sigmoid(mh_ref[...]) * yh + _sigmoid(mc_ref[...]) * yc).astype(o_ref.dtype)


def _merge_proj(o_hg, w_h, o_cv, w_c, layer, p, gate_col0, tm=1024, tn=512):
    m, kh = o_hg.shape
    kc = o_cv.shape[1]
    n = w_h.shape[2]
    tm = min(tm, m)
    gh = gate_col0 // tn
    gc = (gate_col0 + n) // tn
    return pl.pallas_call(
        _merge_kernel,
        grid=(m // tm, n // tn),
        in_specs=[pl.BlockSpec((tm, kh), lambda i, j: (i, 0)),
                  pl.BlockSpec((None, kh, tn), lambda i, j: (layer, 0, j)),
                  pl.BlockSpec((tm, kc), lambda i, j: (i, 0)),
                  pl.BlockSpec((None, kc, tn), lambda i, j: (layer, 0, j)),
                  pl.BlockSpec((tm, tn), lambda i, j: (i, gh + j)),
                  pl.BlockSpec((tm, tn), lambda i, j: (i, gc + j))],
        out_specs=pl.BlockSpec((tm, tn), lambda i, j: (i, j)),
        out_shape=jax.ShapeDtypeStruct((m, n), BF16),
        compiler_params=_params(2),
        name="merge_proj",
    )(o_hg, w_h, o_cv, w_c, p, p)


def _out_kernel(a_ref, w_ref, x_ref, gate_ref, o_ref):
    y = jnp.dot(a_ref[...], w_ref[...], preferred_element_type=F32)
    o_ref[...] = x_ref[...] + gate_ref[...] * y


def _out_proj(merged, w, layer, x, mod, rows_per_batch, row_of_batch, tm=1024, tn=512):
    m, k = merged.shape
    d = w.shape[2]
    tm = min(tm, m, rows_per_batch)
    blocks_per_batch = rows_per_batch // tm
    gate0 = 2 * d // tn
    return pl.pallas_call(
        _out_kernel,
        grid=(m // tm, d // tn),
        in_specs=[pl.BlockSpec((tm, k), lambda i, j: (i, 0)),
                  pl.BlockSpec((None, k, tn), lambda i, j: (layer, 0, j)),
                  pl.BlockSpec((tm, tn), lambda i, j: (i, j)),
                  pl.BlockSpec((None, 1, tn),
                               lambda i, j: (row_of_batch(i // blocks_per_batch), 0, gate0 + j))],
        out_specs=pl.BlockSpec((tm, tn), lambda i, j: (i, j)),
        out_shape=jax.ShapeDtypeStruct((m, d), F32),
        compiler_params=_params(2),
        name="out_proj",
    )(merged, w, x, mod)


def _dot_nt(a, b):
    return lax.dot_general(a, b, (((1,), (1,)), ((), ())), preferred_element_type=F32)


def _cumsum_rows(x, reverse):
    n = x.shape[0]
    row = lax.broadcasted_iota(jnp.int32, x.shape, 0)
    s = 1
    while s < n:
        if reverse:
            x = x + jnp.where(row < n - s, pltpu.roll(x, n - s, axis=0), 0.0)
        else:
            x = x + jnp.where(row >= s, pltpu.roll(x, s, axis=0), 0.0)
        s *= 2
    return x


def _scan_chunk(q_raw, z, val, la, oml, st_ref, sc_ref, o_ref, q_s, k_s, b_s, v_s, reverse):
    n = q_raw.shape[0]
    q = q_raw * _sigmoid(q_raw)
    logf = (jnp.maximum(la, z) - jnp.maximum(z, 0.0)
            + jnp.log(1.0 + jnp.exp(-jnp.abs(la - z))) - jnp.log(1.0 + jnp.exp(-jnp.abs(z))))
    k = oml * _sigmoid(-z)
    b = _cumsum_rows(logf, reverse)
    q_s[...] = q
    k_s[...] = k
    b_s[...] = b
    v_s[...] = val
    bound = b_s[0:1, :] if reverse else b_s[n - 1:n, :]

    st = st_ref[...]
    v16 = val.astype(BF16)
    o = _dot_nt((q * jnp.exp(b)).astype(BF16), st.astype(BF16))
    kd = (k * jnp.exp(bound - b)).astype(BF16)
    st_ref[...] = st * jnp.exp(bound) + jnp.dot(val.T.astype(BF16), kd, preferred_element_type=F32)

    lane = lax.broadcasted_iota(jnp.int32, (SUBLANES, LANES), 1)
    m = n // 2
    while m >= SCAN_DIAG:
        groups = n // (2 * m)
        t_off = 0 if reverse else m
        s_off = m if reverse else 0
        r_off = m - 1 if reverse else m
        bref = jnp.concatenate(
            [jnp.broadcast_to(b_s[g * 2 * m + r_off:g * 2 * m + r_off + 1, :], (2 * m, LANES))
             for g in range(groups)], axis=0)
        ew = jnp.exp(-jnp.abs(b_s[...] - bref))
        qe = q_s[...] * ew
        ke = (k_s[...] * ew).astype(BF16)
        qt = jnp.concatenate([qe[g * 2 * m + t_off:g * 2 * m + t_off + m] for g in range(groups)],
                             axis=0).astype(BF16)
        g_all = _dot_nt(qt, ke)
        for g in range(groups):
            t0 = g * 2 * m + t_off
            s0 = g * 2 * m + s_off
            tile0 = (s0 // LANES) * LANES
            if m >= LANES:
                sc_ref[t0:t0 + m, s0:s0 + m] = g_all[g * m:(g + 1) * m, s0:s0 + m]
            else:
                keep = (lane >= s0 - tile0) & (lane < s0 - tile0 + m)
                for i in range(m // SUBLANES):
                    rows = slice(t0 + i * SUBLANES, t0 + (i + 1) * SUBLANES)
                    new = g_all[g * m + i * SUBLANES:g * m + (i + 1) * SUBLANES, tile0:tile0 + LANES]
                    sc_ref[rows, tile0:tile0 + LANES] = jnp.where(
                        keep, new, sc_ref[rows, tile0:tile0 + LANES])
        m //= 2

    o = o + jnp.dot(sc_ref[...].astype(BF16), v16, preferred_element_type=F32)

    trow = lax.broadcasted_iota(jnp.int32, (SCAN_DIAG, LANES), 0)
    diag = []
    for t0 in range(0, n, SCAN_DIAG):
        qb = q_s[t0:t0 + SCAN_DIAG, :]
        bb = b_s[t0:t0 + SCAN_DIAG, :]
        acc = None
        for s in range(SCAN_DIAG):
            ks = k_s[t0 + s:t0 + s + 1, :]
            bs = b_s[t0 + s:t0 + s + 1, :]
            vs = v_s[t0 + s:t0 + s + 1, :]
            w = qb * ks * jnp.exp(bb - bs)
            if 0 < s < SCAN_DIAG - 1 or (s == 0 and reverse) or (s == SCAN_DIAG - 1 and not reverse):
                w = jnp.where((trow <= s) if reverse else (trow >= s), w, 0.0)
            term = jnp.sum(w, axis=-1, keepdims=True) * vs
            acc = term if acc is None else acc + term
        diag.append(acc)
    o_ref[...] = o + jnp.concatenate(diag, axis=0)


def _scan_kernel(*refs, n_chunks, hb, has_init, want_final, has_mm):
    lb_ref, qf_ref, vf_ref, zf_ref, qb_ref, vb_ref, zb_ref = refs[:7]
    pos = 7
    s0_ref = None
    if has_init:
        s0_ref = refs[pos]
        pos += 1
    if has_mm:
        a_ref, w_ref = refs[pos:pos + 2]
        pos += 2
    of_ref, ob_ref = refs[pos:pos + 2]
    pos += 2
    sfin_ref = None
    if want_final:
        sfin_ref = refs[pos]
        pos += 1
    if has_mm:
        mm_ref = refs[pos]
        pos += 1
    st_ref, sc_ref, tmp_ref = refs[pos:]
    c = pl.program_id(2)

    @pl.when(c == 0)
    def _():
        sc_ref[...] = jnp.zeros(sc_ref.shape, F32)
        for d in range(2):
            for h in range(hb):
                if has_init:
                    st_ref[d, h] = s0_ref[d, h].T
                else:
                    st_ref[d, h] = jnp.zeros((HEAD_DIM, HEAD_DIM), F32)

    for h in range(hb):
        lanes = slice(h * HEAD_DIM, (h + 1) * HEAD_DIM)
        for d, (q_ref, v_ref, z_ref, o_ref) in enumerate(
                ((qf_ref, vf_ref, zf_ref, of_ref), (qb_ref, vb_ref, zb_ref, ob_ref))):
            if has_mm:
                part = a_ref.shape[0] // (2 * hb)
                rows = slice((2 * h + d) * part, (2 * h + d + 1) * part)
                mm_ref[rows, :] = jnp.dot(a_ref[rows, :], w_ref[...], preferred_element_type=F32)
            _scan_chunk(q_ref[:, lanes], z_ref[:, lanes], v_ref[:, lanes],
                        lb_ref[2 * d:2 * d + 1, lanes], lb_ref[2 * d + 1:2 * d + 2, lanes],
                        st_ref.at[d, h], sc_ref.at[d, h], o_ref.at[:, lanes],
                        *(tmp_ref.at[d, h, i] for i in range(4)), reverse=(d == 1))

    if want_final:
        @pl.when(c == n_chunks - 1)
        def _():
            for d in range(2):
                for h in range(hb):
                    sfin_ref[d, h] = st_ref[d, h].T


def _rider_tiles(steps, m, n):
    for tn in (768, 1024, 512, 1536, 2048, 3072, 256, 384):
        if n % tn:
            continue
        n_col = n // tn
        if steps % n_col or m % (steps // n_col):
            continue
        tm = m // (steps // n_col)
        if tm % (2 * SUBLANES) == 0 and tm <= 1024:
            return tm, tn
    return None


def _hgrn_scan(p, lb_terms, state, state_b0, layer, want_final, rider=None):
    bsz, l, _ = p.shape
    d_hg = lb_terms.shape[1]
    heads = d_hg // HEAD_DIM
    hb = SCAN_HEADS_PER_STEP
    w = hb * HEAD_DIM
    n_chunks = l // SCAN_CHUNK
    n_groups = heads // hb
    seg = d_hg // w
    has_init = state is not None

    def fwd(col):
        return pl.BlockSpec((None, SCAN_CHUNK, w), lambda b, g, c: (b, c, col * seg + g))

    def bwd(col):
        return pl.BlockSpec((None, SCAN_CHUNK, w), lambda b, g, c: (b, n_chunks - 1 - c, col * seg + g))

    in_specs = [pl.BlockSpec((4, w), lambda b, g, c: (0, g)),
                fwd(0), fwd(3), fwd(1), bwd(0), bwd(3), bwd(2)]
    args = [lb_terms, p, p, p, p, p, p]
    if has_init:
        in_specs.append(pl.BlockSpec((None, None, 2, hb, HEAD_DIM, HEAD_DIM),
                                     lambda b, g, c: (state_b0 + b, layer, 0, g, 0, 0)))
        args.append(state)
    out_specs = [pl.BlockSpec((None, SCAN_CHUNK, w), lambda b, g, c: (b, c, g)),
                 pl.BlockSpec((None, SCAN_CHUNK, w), lambda b, g, c: (b, n_chunks - 1 - c, g))]
    out_shape = [jax.ShapeDtypeStruct((bsz, l, d_hg), F32)] * 2
    if want_final:
        out_specs.append(pl.BlockSpec((None, 2, hb, HEAD_DIM, HEAD_DIM), lambda b, g, c: (b, 0, g, 0, 0)))
        out_shape.append(jax.ShapeDtypeStruct((bsz, 2, heads, HEAD_DIM, HEAD_DIM), F32))
    if rider is not None:
        a, wr = rider
        m, k = a.shape
        n = wr.shape[2]
        tm, tn = _rider_tiles(bsz * n_groups * n_chunks, m, n)
        n_col = n // tn

        def step(b, g, c):
            return (b * n_groups + g) * n_chunks + c

        in_specs += [pl.BlockSpec((tm, k), lambda b, g, c: (step(b, g, c) // n_col, 0)),
                     pl.BlockSpec((None, k, tn), lambda b, g, c: (layer, 0, step(b, g, c) % n_col))]
        args += [a, wr]
        out_specs.append(pl.BlockSpec((tm, tn), lambda b, g, c: (step(b, g, c) // n_col, step(b, g, c) % n_col)))
        out_shape.append(jax.ShapeDtypeStruct((m, n), F32))
    scratch = [pltpu.VMEM((2, hb, HEAD_DIM, HEAD_DIM), F32),
               pltpu.VMEM((2, hb, SCAN_CHUNK, SCAN_CHUNK), F32),
               pltpu.VMEM((2, hb, 4, SCAN_CHUNK, HEAD_DIM), F32)]
    return pl.pallas_call(
        functools.partial(_scan_kernel, n_chunks=n_chunks, hb=hb, has_init=has_init, want_final=want_final,
                          has_mm=rider is not None),
        grid=(bsz, n_groups, n_chunks),
        in_specs=in_specs,
        out_specs=out_specs,
        out_shape=out_shape,
        scratch_shapes=scratch,
        compiler_params=_params(3),
        name="hgrn_scan",
    )(*args)


def _hg_post_kernel(of_ref, ob_ref, g_ref, ng_ref, o_ref):
    heads = of_ref.shape[-1] // HEAD_DIM
    for h in range(heads):
        lanes = slice(h * HEAD_DIM, (h + 1) * HEAD_DIM)
        o = of_ref[:, lanes] + ob_ref[:, lanes]
        o = o * lax.rsqrt(jnp.mean(o * o, axis=-1, keepdims=True) + EPS) * ng_ref[...]
        o_ref[:, lanes] = (o * _silu(g_ref[:, lanes])).astype(o_ref.dtype)


def _hg_post(o_f, o_b, p, norm_g, tl=256):
    b, l, d_hg = o_f.shape
    blk = pl.BlockSpec((None, tl, d_hg), lambda i, j: (i, j, 0))
    return pl.pallas_call(
        _hg_post_kernel,
        grid=(b, l // tl),
        in_specs=[blk, blk,
                  pl.BlockSpec((None, tl, d_hg), lambda i, j: (i, j, 4)),
                  pl.BlockSpec((1, HEAD_DIM), lambda i, j: (0, 0))],
        out_specs=blk,
        out_shape=jax.ShapeDtypeStruct((b, l, d_hg), BF16),
        compiler_params=_params(2),
        name="hg_post",
    )(o_f, o_b, p, norm_g.reshape(1, HEAD_DIM))


def _conv_seq_kernel(a_ref, b_ref, w_ref, bias_ref, y_ref, pad_ref, *, seg):
    tokens, ch = a_ref.shape
    halo = 2 * SUBLANES
    stride = seg + 2 * halo
    sub = min(seg, CONV_ROWS)
    u = a_ref[...] * _sigmoid(b_ref[...])
    for r in range(tokens // seg):
        base = r * stride
        pad_ref[base:base + halo, :] = jnp.zeros((halo, ch), F32)
        pad_ref[base + halo:base + halo + seg, :] = u[r * seg:(r + 1) * seg]
        pad_ref[base + halo + seg:base + stride, :] = jnp.zeros((halo, ch), F32)
    for r in range(tokens // seg):
        for t0 in range(0, seg, sub):
            base = r * stride + halo - CONV_HALF + t0
            for c0 in range(0, ch, LANES):
                acc = jnp.zeros((sub, LANES), F32) + bias_ref[:, c0:c0 + LANES]
                first = base - (base % SUBLANES)
                rows = sub + 2 * halo
                x_al = pad_ref[first:first + rows, c0:c0 + LANES]
                for sh in range(SUBLANES):
                    off = base - first + sh
                    u_sh = x_al if off == 0 else pltpu.roll(x_al, rows - off, axis=0)
                    for a, j in enumerate(range(sh, CONV_TAPS, SUBLANES)):
                        acc = acc + u_sh[a * SUBLANES:a * SUBLANES + sub] * w_ref[j:j + 1, c0:c0 + LANES]
                y_ref[r * seg + t0:r * seg + t0 + sub, c0:c0 + LANES] = acc


def _conv_seq(p, w, bias, seg, tokens=256, tc=512):
    bsz, l, _ = p.shape
    ch = w.shape[1]
    tc = min(tc, ch)
    seg_blocks = ch // tc
    halo = 2 * SUBLANES
    return pl.pallas_call(
        functools.partial(_conv_seq_kernel, seg=seg),
        grid=(bsz, l // tokens, ch // tc),
        in_specs=[pl.BlockSpec((None, tokens, tc), lambda b, t, c: (b, t, 5 * seg_blocks + c)),
                  pl.BlockSpec((None, tokens, tc), lambda b, t, c: (b, t, 6 * seg_blocks + c)),
                  pl.BlockSpec((CONV_TAPS, tc), lambda b, t, c: (0, c)),
                  pl.BlockSpec((1, tc), lambda b, t, c: (0, c))],
        out_specs=pl.BlockSpec((None, tokens, tc), lambda b, t, c: (b, t, c)),
        out_shape=jax.ShapeDtypeStruct((bsz, l, ch), F32),
        scratch_shapes=[pltpu.VMEM(((tokens // seg) * (seg + 2 * halo), tc), F32)],
        compiler_params=_params(3),
        name="conv_seq",
    )(p, p, w, bias.reshape(1, ch))


def _conv_col_kernel(a_ref, b_ref, w_ref, bias_ref, y_ref, pad_ref, *, width):
    tokens, ch = a_ref.shape
    halo = CONV_HALF * width
    pad_ref[0:halo, :] = jnp.zeros((halo, ch), F32)
    pad_ref[halo:halo + tokens, :] = a_ref[...] * _sigmoid(b_ref[...])
    pad_ref[halo + tokens:halo + tokens + halo, :] = jnp.zeros((halo, ch), F32)

    def row(r, carry):
        t0 = pl.multiple_of(r * width, width)
        for c0 in range(0, ch, LANES):
            acc = jnp.zeros((width, LANES), F32) + bias_ref[:, c0:c0 + LANES]
            for j in range(CONV_TAPS):
                acc = acc + pad_ref[pl.ds(t0 + j * width, width), c0:c0 + LANES] * w_ref[j:j + 1, c0:c0 + LANES]
            y_ref[pl.ds(t0, width), c0:c0 + LANES] = acc
        return carry

    lax.fori_loop(0, tokens // width, row, 0)


def _conv_col(p, w, bias, width, tc=256):
    bsz, l, _ = p.shape
    ch = w.shape[1]
    seg_blocks = ch // tc
    return pl.pallas_call(
        functools.partial(_conv_col_kernel, width=width),
        grid=(bsz, ch // tc),
        in_specs=[pl.BlockSpec((None, l, tc), lambda b, c: (b, 0, 5 * seg_blocks + c)),
                  pl.BlockSpec((None, l, tc), lambda b, c: (b, 0, 6 * seg_blocks + c)),
                  pl.BlockSpec((CONV_TAPS, tc), lambda b, c: (0, c)),
                  pl.BlockSpec((1, tc), lambda b, c: (0, c))],
        out_specs=pl.BlockSpec((None, l, tc), lambda b, c: (b, 0, c)),
        out_shape=jax.ShapeDtypeStruct((bsz, l, ch), F32),
        scratch_shapes=[pltpu.VMEM((l + 2 * CONV_HALF * width, tc), F32)],
        compiler_params=_params(2),
        name="conv_col",
    )(p, p, w, bias.reshape(1, ch))


def _cv_post_kernel(y_ref, g_ref, lg_ref, lb_ref, o_ref):
    y = y_ref[...]
    mu = jnp.mean(y, axis=-1, keepdims=True)
    yc = y - mu
    yn = yc * lax.rsqrt(jnp.mean(yc * yc, axis=-1, keepdims=True) + EPS) * lg_ref[...] + lb_ref[...]
    o_ref[...] = (_silu(yn) * _silu(g_ref[...])).astype(o_ref.dtype)


def _cv_post(y, p, ln_g, ln_b, tl=256):
    b, l, ch = y.shape
    blk = pl.BlockSpec((None, tl, ch), lambda i, j: (i, j, 0))
    vec = pl.BlockSpec((1, ch), lambda i, j: (0, 0))
    return pl.pallas_call(
        _cv_post_kernel,
        grid=(b, l // tl),
        in_specs=[blk, pl.BlockSpec((None, tl, ch), lambda i, j: (i, j, 7)), vec, vec],
        out_specs=blk,
        out_shape=jax.ShapeDtypeStruct((b, l, ch), BF16),
        compiler_params=_params(2),
        name="cv_post",
    )(y, p, ln_g.reshape(1, ch), ln_b.reshape(1, ch))


def _layer(streams, layer, mod, lb_terms, norm_g, w_in, hg_norm_g, w_dw, b_dw, ln_g, ln_b,
           w_hproj, w_cproj, w_out):
    d = streams[0]["x"].shape[-1]
    d_hg = w_hproj.shape[1]
    hs = [_prenorm(s["x"], norm_g, mod, s["row"]).reshape(-1, d) for s in streams]
    p = _in_proj(hs[0], w_in, layer)
    new_x, finals = [], []
    scans, ps = [], []
    for i, s in enumerate(streams):
        bsz, l, _ = s["x"].shape
        rider = None
        if i + 1 < len(streams):
            steps = bsz * (d_hg // (SCAN_HEADS_PER_STEP * HEAD_DIM)) * (l // SCAN_CHUNK)
            if _rider_tiles(steps, hs[i + 1].shape[0], w_in.shape[2]) is not None:
                rider = (hs[i + 1], w_in)
        res = list(_hgrn_scan(p.reshape(bsz, l, -1), lb_terms, s["state"], s["state_b0"], layer, s["final"],
                              rider))
        ps.append(p)
        scans.append(res[:2])
        finals.append(res[2] if s["final"] else None)
        if rider is not None:
            p = res[-1]
        elif i + 1 < len(streams):
            p = _in_proj(hs[i + 1], w_in, layer)
    for s, p, (o_f, o_b) in zip(streams, ps, scans):
        bsz, l, _ = s["x"].shape
        p3 = p.reshape(bsz, l, -1)
        o_hg = _hg_post(o_f, o_b, p3, hg_norm_g)
        if s["conv"] == "col":
            y = _conv_col(p3, w_dw, b_dw, LATENT_GRID_W)
        else:
            y = _conv_seq(p3, w_dw, b_dw, LATENT_GRID_W if s["conv"] == "row" else l)
        o_cv = _cv_post(y, p3, ln_g, ln_b)
        merged = _merge_proj(o_hg.reshape(bsz * l, d_hg), w_hproj, o_cv.reshape(bsz * l, -1), w_cproj,
                             layer, p, 4 * d)
        x_new = _out_proj(merged, w_out, layer, s["x"].reshape(bsz * l, d), mod, l, s["row"])
        new_x.append(x_new.reshape(bsz, l, d))
    return new_x, finals


def kernel(x_prompt, x_sample, state_hgrn, c, c_ctx, mod_w, mod_b, norm_g, w_in, hg_lb_logits, hg_norm_g,
           cv_dw_w, cv_dw_b, cv_ln_g, cv_ln_b, w_hproj, w_cproj, w_out, final_g):
    depth = mod_w.shape[0]
    d = x_prompt.shape[-1]
    dec_batch = c.shape[0]
    rows = 1 + dec_batch
    pad = (-rows) % SUBLANES
    cond = jnp.concatenate([c_ctx.reshape(1, d), c, jnp.zeros((pad, d), F32)], axis=0).astype(F32)
    mod = _modulation(cond, mod_w, mod_b)
    mod = mod.reshape(depth, rows + pad, 1, 3 * d)
    lb = _lower_bound_terms(hg_lb_logits)
    d_hg = hg_lb_logits.shape[-1]
    lb = lb.reshape(2, depth, 2, d_hg)

    w_in16 = w_in.astype(BF16)
    w_h16 = w_hproj.astype(BF16)
    w_c16 = w_cproj.astype(BF16)
    w_o16 = w_out.astype(BF16)

    parts = 2 if dec_batch % 2 == 0 else 1
    per = dec_batch // parts
    streams = []
    for i in range(parts):
        streams.append(dict(x=x_sample[i * per:(i + 1) * per], row=functools.partial(lambda b, o: 1 + o + b, o=i * per),
                            state=state_hgrn, state_b0=i * per, final=False, conv=None))
    streams.append(dict(x=x_prompt, row=lambda b: 0, state=None, state_b0=0, final=True, conv=None))
    states = []
    for l in range(depth):
        lb_l = jnp.transpose(lb[:, l], (1, 0, 2)).reshape(4, d_hg)
        for s in streams[:parts]:
            s["conv"] = "row" if l % 2 == 0 else "col"
        xs, finals = _layer(streams, l, mod[l], lb_l, norm_g[l], w_in16, hg_norm_g[l], cv_dw_w[l], cv_dw_b[l],
                            cv_ln_g[l], cv_ln_b[l], w_h16, w_c16, w_o16)
        for s, x in zip(streams, xs):
            s["x"] = x
        states.append(finals[-1])
    y_prompt = _final_norm(streams[-1]["x"], final_g)
    y_sample = jnp.concatenate([_final_norm(s["x"], final_g) for s in streams[:parts]], axis=0)
    return y_prompt, y_sample, jnp.stack(states, axis=1)
```

```python
import functools

import jax
import jax.numpy as jnp
from jax import lax
from jax.experimental import pallas as pl
from jax.experimental.pallas import tpu as pltpu

F32 = jnp.float32
BF16 = jnp.bfloat16

EPS = 1e-6
LOG2_E = 1.4426950408889634
HEAD_DIM = 128
CONV_TAPS = 31
CONV_HALF = CONV_TAPS // 2
CONV_ROWS = 64
LATENT_GRID_W = 64
SCAN_CHUNK = 256
SCAN_DIAG = 8
SCAN_HEADS_PER_STEP = 2
LANES = 128
SUBLANES = 8
VMEM_LIMIT_BYTES = 56 * 1024 * 1024


def _params(n_axes):
    return pltpu.CompilerParams(dimension_semantics=("arbitrary",) * n_axes,
                                vmem_limit_bytes=VMEM_LIMIT_BYTES)


def _sigmoid(x):
    return 0.5 * jnp.tanh(0.5 * x) + 0.5


def _silu(x):
    return x * _sigmoid(x)


def _lb_kernel(logit_ref, out_ref):
    depth = logit_ref.shape[0]
    x = logit_ref[...]
    mx = jnp.max(x, axis=0, keepdims=True)
    ex = jnp.exp(x - mx)
    p = ex / jnp.sum(ex, axis=0, keepdims=True)
    p0 = p[0:1]
    cum = p0
    for l in range(depth):
        if l > 0:
            cum = cum + p[l:l + 1]
        lb = cum - p0
        out_ref[0, l:l + 1, :] = jnp.log2(lb)
        out_ref[1, l:l + 1, :] = 1.0 - lb


def _lower_bound_terms(lb_logits):
    depth = lb_logits.shape[0]
    flat = lb_logits.reshape(depth, -1).astype(F32)
    return pl.pallas_call(
        _lb_kernel,
        out_shape=jax.ShapeDtypeStruct((2, depth, flat.shape[1]), F32),
        name="lb_terms",
    )(flat)


def _mod_kernel(c_ref, w_ref, b_ref, o_ref):
    a = _silu(c_ref[...]).astype(BF16)
    w = w_ref[...].astype(BF16)
    o_ref[...] = jnp.dot(a, w, preferred_element_type=F32) + b_ref[...]


def _modulation(cond, mod_w, mod_b, tn=512):
    depth, d, n = mod_w.shape
    r = cond.shape[0]
    return pl.pallas_call(
        _mod_kernel,
        grid=(depth, n // tn),
        in_specs=[pl.BlockSpec((r, d), lambda l, j: (0, 0)),
                  pl.BlockSpec((None, d, tn), lambda l, j: (l, 0, j)),
                  pl.BlockSpec((None, 1, tn), lambda l, j: (l, 0, j))],
        out_specs=pl.BlockSpec((None, r, tn), lambda l, j: (l, 0, j)),
        out_shape=jax.ShapeDtypeStruct((depth, r, n), F32),
        compiler_params=_params(2),
        name="adaln_mod",
    )(cond, mod_w, mod_b.reshape(depth, 1, n))


def _prenorm_kernel(x_ref, g_ref, mod_ref, h_ref):
    d = x_ref.shape[-1]
    x = x_ref[...]
    y = x * lax.rsqrt(jnp.mean(x * x, axis=-1, keepdims=True) + EPS) * g_ref[...]
    shift = mod_ref[:, 0:d]
    scale = mod_ref[:, d:2 * d]
    h_ref[...] = (y * (1.0 + scale) + shift).astype(h_ref.dtype)


def _prenorm(x, b0, b, g, mod, row_of_batch, tl=256):
    _, l, d = x.shape
    return pl.pallas_call(
        _prenorm_kernel,
        grid=(b, l // tl),
        in_specs=[pl.BlockSpec((None, tl, d), lambda i, j: (b0 + i, j, 0)),
                  pl.BlockSpec((1, d), lambda i, j: (0, 0)),
                  pl.BlockSpec((None, 1, 3 * d), lambda i, j: (row_of_batch(i), 0, 0))],
        out_specs=pl.BlockSpec((None, tl, d), lambda i, j: (i, j, 0)),
        out_shape=jax.ShapeDtypeStruct((b, l, d), BF16),
        compiler_params=_params(2),
        name="prenorm",
    )(x, g.reshape(1, d), mod)


def _final_norm_kernel(x_ref, g_ref, o_ref):
    x = x_ref[...]
    o_ref[...] = x * lax.rsqrt(jnp.mean(x * x, axis=-1, keepdims=True) + EPS) * g_ref[...]


def _final_norm(x, g, tl=256):
    b, l, d = x.shape
    return pl.pallas_call(
        _final_norm_kernel,
        grid=(b, l // tl),
        in_specs=[pl.BlockSpec((None, tl, d), lambda i, j: (i, j, 0)),
                  pl.BlockSpec((1, d), lambda i, j: (0, 0))],
        out_specs=pl.BlockSpec((None, tl, d), lambda i, j: (i, j, 0)),
        out_shape=jax.ShapeDtypeStruct((b, l, d), F32),
        compiler_params=_params(2),
        name="final_norm",
    )(x, g.reshape(1, d))


def _matmul_kernel(a_ref, w_ref, o_ref):
    o_ref[...] = jnp.dot(a_ref[...], w_ref[...], preferred_element_type=F32).astype(o_ref.dtype)


def _in_proj(h, w, layer, tm=1024, tn=1024):
    m, k = h.shape
    n = w.shape[2]
    tm = min(tm, m)
    return pl.pallas_call(
        _matmul_kernel,
        grid=(m // tm, n // tn),
        in_specs=[pl.BlockSpec((tm, k), lambda i, j: (i, 0)),
                  pl.BlockSpec((None, k, tn), lambda i, j: (layer, 0, j))],
        out_specs=pl.BlockSpec((tm, tn), lambda i, j: (i, j)),
        out_shape=jax.ShapeDtypeStruct((m, n), F32),
        compiler_params=_params(2),
        name="in_proj",
    )(h, w)


def _merge_kernel(oh_ref, wh_ref, oc_ref, wc_ref, mh_ref, mc_ref, o_ref):
    yh = jnp.dot(oh_ref[...], wh_ref[...], preferred_element_type=F32)
    yc = jnp.dot(oc_ref[...], wc_ref[...], preferred_element_type=F32)
    o_ref[...] = (_sigmoid(mh_ref[...]) * yh + _sigmoid(mc_ref[...]) * yc).astype(o_ref.dtype)


def _merge_proj(o_hg, w_h, o_cv, w_c, layer, p, gate_col0, tm=1024, tn=512):
    m, kh = o_hg.shape
    kc = o_cv.shape[1]
    n = w_h.shape[2]
    tm = min(tm, m)
    tn = min(tn, n)
    gh = gate_col0 // tn
    gc = (gate_col0 + n) // tn
    return pl.pallas_call(
        _merge_kernel,
        grid=(m // tm, n // tn),
        in_specs=[pl.BlockSpec((tm, kh), lambda i, j: (i, 0)),
                  pl.BlockSpec((None, kh, tn), lambda i, j: (layer, 0, j)),
                  pl.BlockSpec((tm, kc), lambda i, j: (i, 0)),
                  pl.BlockSpec((None, kc, tn), lambda i, j: (layer, 0, j)),
                  pl.BlockSpec((tm, tn), lambda i, j: (i, gh + j)),
                  pl.BlockSpec((tm, tn), lambda i, j: (i, gc + j))],
        out_specs=pl.BlockSpec((tm, tn), lambda i, j: (i, j)),
        out_shape=jax.ShapeDtypeStruct((m, n), BF16),
        compiler_params=_params(2),
        name="merge_proj",
    )(o_hg, w_h, o_cv, w_c, p, p)


def _out_kernel(a_ref, w_ref, x_ref, gate_ref, o_ref):
    y = jnp.dot(a_ref[...], w_ref[...], preferred_element_type=F32)
    o_ref[...] = x_ref[...] + gate_ref[...] * y


def _out_proj(merged, w, layer, x, x_row0, mod, rows_per_batch, row_of_batch, tm=1024, tn=1024):
    m, k = merged.shape
    d = w.shape[2]
    tm = min(tm, m, rows_per_batch)
    tn = min(tn, d)
    blocks_per_batch = rows_per_batch // tm
    gate0 = 2 * d // tn
    x_blk0 = x_row0 // tm
    return pl.pallas_call(
        _out_kernel,
        grid=(m // tm, d // tn),
        in_specs=[pl.BlockSpec((tm, k), lambda i, j: (i, 0)),
                  pl.BlockSpec((None, k, tn), lambda i, j: (layer, 0, j)),
                  pl.BlockSpec((tm, tn), lambda i, j: (x_blk0 + i, j)),
                  pl.BlockSpec((None, 1, tn),
                               lambda i, j: (row_of_batch(i // blocks_per_batch), 0, gate0 + j))],
        out_specs=pl.BlockSpec((tm, tn), lambda i, j: (i, j)),
        out_shape=jax.ShapeDtypeStruct((m, d), F32),
        compiler_params=_params(2),
        name="out_proj",
    )(merged, w, x, mod)


def _dot_nt(a, b):
    return lax.dot_general(a, b, (((1,), (1,)), ((), ())), preferred_element_type=F32)


def _cumsum_rows(x, reverse):
    n = x.shape[0]
    row = lax.broadcasted_iota(jnp.int32, x.shape, 0)
    s = 1
    while s < n:
        if reverse:
            x = x + jnp.where(row < n - s, pltpu.roll(x, n - s, axis=0), 0.0)
        else:
            x = x + jnp.where(row >= s, pltpu.roll(x, s, axis=0), 0.0)
        s *= 2
    return x


def _scan_chunk(q_raw, z, val, la, oml, st_ref, sc_ref, o_ref, q_s, k_s, b_s, v_s, reverse):
    n = q_raw.shape[0]
    q = q_raw * _sigmoid(q_raw)
    z2 = z * LOG2_E
    logf = (jnp.maximum(la, z2) - jnp.maximum(z2, 0.0)
            + jnp.log2(1.0 + jnp.exp2(-jnp.abs(la - z2))) - jnp.log2(1.0 + jnp.exp2(-jnp.abs(z2))))
    k = oml * _sigmoid(-z)
    b = _cumsum_rows(logf, reverse)
    q_s[...] = q
    k_s[...] = k
    b_s[...] = b
    v_s[...] = val
    bound = b_s[0:1, :] if reverse else b_s[n - 1:n, :]

    st = st_ref[...]
    v16 = val.astype(BF16)
    o = _dot_nt((q * jnp.exp2(b)).astype(BF16), st.astype(BF16))
    kd = (k * jnp.exp2(bound - b)).astype(BF16)
    st_ref[...] = st * jnp.exp2(bound) + jnp.dot(val.T.astype(BF16), kd, preferred_element_type=F32)

    lane = lax.broadcasted_iota(jnp.int32, (SUBLANES, LANES), 1)
    m = n // 2
    while m >= SCAN_DIAG:
        groups = n // (2 * m)
        t_off = 0 if reverse else m
        s_off = m if reverse else 0
        r_off = m - 1 if reverse else m
        args = []
        for g in range(groups):
            lo = b_s[g * 2 * m:g * 2 * m + m, :]
            hi = b_s[g * 2 * m + m:(g + 1) * 2 * m, :]
            ref = b_s[g * 2 * m + r_off:g * 2 * m + r_off + 1, :]
            args += [lo - ref, ref - hi] if reverse else [ref - lo, hi - ref]
        ew = jnp.exp2(jnp.concatenate(args, axis=0))
        qe = q_s[...] * ew
        ke = (k_s[...] * ew).astype(BF16)
        qt = jnp.concatenate([qe[g * 2 * m + t_off:g * 2 * m + t_off + m] for g in range(groups)],
                             axis=0).astype(BF16)
        g_all = _dot_nt(qt, ke)
        for g in range(groups):
            t0 = g * 2 * m + t_off
            s0 = g * 2 * m + s_off
            tile0 = (s0 // LANES) * LANES
            if m >= LANES:
                sc_ref[t0:t0 + m, s0:s0 + m] = g_all[g * m:(g + 1) * m, s0:s0 + m]
            else:
                keep = (lane >= s0 - tile0) & (lane < s0 - tile0 + m)
                for i in range(m // SUBLANES):
                    rows = slice(t0 + i * SUBLANES, t0 + (i + 1) * SUBLANES)
                    new = g_all[g * m + i * SUBLANES:g * m + (i + 1) * SUBLANES, tile0:tile0 + LANES]
                    sc_ref[rows, tile0:tile0 + LANES] = jnp.where(
                        keep, new, sc_ref[rows, tile0:tile0 + LANES])
        m //= 2

    o = o + jnp.dot(sc_ref[...].astype(BF16), v16, preferred_element_type=F32)

    trow = lax.broadcasted_iota(jnp.int32, (SCAN_DIAG, LANES), 0)
    diag = []
    for t0 in range(0, n, SCAN_DIAG):
        qb = q_s[t0:t0 + SCAN_DIAG, :]
        bb = b_s[t0:t0 + SCAN_DIAG, :]
        acc = None
        for s in range(SCAN_DIAG):
            ks = k_s[t0 + s:t0 + s + 1, :]
            bs = b_s[t0 + s:t0 + s + 1, :]
            vs = v_s[t0 + s:t0 + s + 1, :]
            w = qb * ks * jnp.exp2(bb - bs)
            if 0 < s < SCAN_DIAG - 1 or (s == 0 and reverse) or (s == SCAN_DIAG - 1 and not reverse):
                w = jnp.where((trow <= s) if reverse else (trow >= s), w, 0.0)
            term = jnp.sum(w, axis=-1, keepdims=True) * vs
            acc = term if acc is None else acc + term
        diag.append(acc)
    o_ref[...] = o + jnp.concatenate(diag, axis=0)


def _scan_kernel(*refs, n_chunks, hb, has_init, want_final, has_mm):
    lb_ref, qf_ref, vf_ref, zf_ref, qb_ref, vb_ref, zb_ref = refs[:7]
    pos = 7
    s0_ref = None
    if has_init:
        s0_ref = refs[pos]
        pos += 1
    if has_mm:
        a_ref, w_ref = refs[pos:pos + 2]
        pos += 2
    of_ref, ob_ref = refs[pos:pos + 2]
    pos += 2
    sfin_ref = None
    if want_final:
        sfin_ref = refs[pos]
        pos += 1
    if has_mm:
        mm_ref = refs[pos]
        pos += 1
    st_ref, sc_ref, tmp_ref = refs[pos:]
    c = pl.program_id(2)

    @pl.when(c == 0)
    def _():
        sc_ref[...] = jnp.zeros(sc_ref.shape, F32)
        for d in range(2):
            for h in range(hb):
                if has_init:
                    st_ref[d, h] = s0_ref[d, h].T
                else:
                    st_ref[d, h] = jnp.zeros((HEAD_DIM, HEAD_DIM), F32)

    for h in range(hb):
        lanes = slice(h * HEAD_DIM, (h + 1) * HEAD_DIM)
        for d, (q_ref, v_ref, z_ref, o_ref) in enumerate(
                ((qf_ref, vf_ref, zf_ref, of_ref), (qb_ref, vb_ref, zb_ref, ob_ref))):
            if has_mm:
                part = a_ref.shape[0] // (2 * hb)
                rows = slice((2 * h + d) * part, (2 * h + d + 1) * part)
                mm_ref[rows, :] = jnp.dot(a_ref[rows, :], w_ref[...], preferred_element_type=F32)
            _scan_chunk(q_ref[:, lanes], z_ref[:, lanes], v_ref[:, lanes],
                        lb_ref[2 * d:2 * d + 1, lanes], lb_ref[2 * d + 1:2 * d + 2, lanes],
                        st_ref.at[d, h], sc_ref.at[d, h], o_ref.at[:, lanes],
                        *(tmp_ref.at[d, h, i] for i in range(4)), reverse=(d == 1))

    if want_final:
        @pl.when(c == n_chunks - 1)
        def _():
            for d in range(2):
                for h in range(hb):
                    sfin_ref[d, h] = st_ref[d, h].T


def _rider_tiles(steps, m, n):
    for tn in (768, 1024, 512, 1536, 2048, 3072, 256, 384):
        if n % tn:
            continue
        n_col = n // tn
        if steps % n_col or m % (steps // n_col):
            continue
        tm = m // (steps // n_col)
        if tm % (2 * SUBLANES) == 0 and tm <= 1024:
            return tm, tn
    return None


def _hgrn_scan(p, lb_terms, state, state_b0, layer, want_final, rider=None):
    bsz, l, _ = p.shape
    d_hg = lb_terms.shape[1]
    heads = d_hg // HEAD_DIM
    hb = SCAN_HEADS_PER_STEP
    w = hb * HEAD_DIM
    n_chunks = l // SCAN_CHUNK
    n_groups = heads // hb
    seg = d_hg // w
    has_init = state is not None

    def fwd(col):
        return pl.BlockSpec((None, SCAN_CHUNK, w), lambda b, g, c: (b, c, col * seg + g))

    def bwd(col):
        return pl.BlockSpec((None, SCAN_CHUNK, w), lambda b, g, c: (b, n_chunks - 1 - c, col * seg + g))

    in_specs = [pl.BlockSpec((4, w), lambda b, g, c: (0, g)),
                fwd(0), fwd(3), fwd(1), bwd(0), bwd(3), bwd(2)]
    args = [lb_terms, p, p, p, p, p, p]
    if has_init:
        in_specs.append(pl.BlockSpec((None, None, 2, hb, HEAD_DIM, HEAD_DIM),
                                     lambda b, g, c: (state_b0 + b, layer, 0, g, 0, 0)))
        args.append(state)
    out_specs = [pl.BlockSpec((None, SCAN_CHUNK, w), lambda b, g, c: (b, c, g)),
                 pl.BlockSpec((None, SCAN_CHUNK, w), lambda b, g, c: (b, n_chunks - 1 - c, g))]
    out_shape = [jax.ShapeDtypeStruct((bsz, l, d_hg), F32)] * 2
    if want_final:
        out_specs.append(pl.BlockSpec((None, 2, hb, HEAD_DIM, HEAD_DIM), lambda b, g, c: (b, 0, g, 0, 0)))
        out_shape.append(jax.ShapeDtypeStruct((bsz, 2, heads, HEAD_DIM, HEAD_DIM), F32))
    if rider is not None:
        a, wr = rider
        m, k = a.shape
        n = wr.shape[2]
        tm, tn = _rider_tiles(bsz * n_groups * n_chunks, m, n)
        n_col = n // tn

        def step(b, g, c):
            return (b * n_groups + g) * n_chunks + c

        in_specs += [pl.BlockSpec((tm, k), lambda b, g, c: (step(b, g, c) // n_col, 0)),
                     pl.BlockSpec((None, k, tn), lambda b, g, c: (layer, 0, step(b, g, c) % n_col))]
        args += [a, wr]
        out_specs.append(pl.BlockSpec((tm, tn), lambda b, g, c: (step(b, g, c) // n_col, step(b, g, c) % n_col)))
        out_shape.append(jax.ShapeDtypeStruct((m, n), F32))
    scratch = [pltpu.VMEM((2, hb, HEAD_DIM, HEAD_DIM), F32),
               pltpu.VMEM((2, hb, SCAN_CHUNK, SCAN_CHUNK), F32),
               pltpu.VMEM((2, hb, 4, SCAN_CHUNK, HEAD_DIM), F32)]
    return pl.pallas_call(
        functools.partial(_scan_kernel, n_chunks=n_chunks, hb=hb, has_init=has_init, want_final=want_final,
                          has_mm=rider is not None),
        grid=(bsz, n_groups, n_chunks),
        in_specs=in_specs,
        out_specs=out_specs,
        out_shape=out_shape,
        scratch_shapes=scratch,
        compiler_params=_params(3),
        name="hgrn_scan",
    )(*args)


def _hg_post_kernel(of_ref, ob_ref, g_ref, ng_ref, o_ref):
    heads = of_ref.shape[-1] // HEAD_DIM
    for h in range(heads):
        lanes = slice(h * HEAD_DIM, (h + 1) * HEAD_DIM)
        o = of_ref[:, lanes] + ob_ref[:, lanes]
        o = o * lax.rsqrt(jnp.mean(o * o, axis=-1, keepdims=True) + EPS) * ng_ref[...]
        o_ref[:, lanes] = (o * _silu(g_ref[:, lanes])).astype(o_ref.dtype)


def _hg_post(o_f, o_b, p, norm_g, tl=256):
    b, l, d_hg = o_f.shape
    blk = pl.BlockSpec((None, tl, d_hg), lambda i, j: (i, j, 0))
    return pl.pallas_call(
        _hg_post_kernel,
        grid=(b, l // tl),
        in_specs=[blk, blk,
                  pl.BlockSpec((None, tl, d_hg), lambda i, j: (i, j, 4)),
                  pl.BlockSpec((1, HEAD_DIM), lambda i, j: (0, 0))],
        out_specs=blk,
        out_shape=jax.ShapeDtypeStruct((b, l, d_hg), BF16),
        compiler_params=_params(2),
        name="hg_post",
    )(o_f, o_b, p, norm_g.reshape(1, HEAD_DIM))


def _conv_seq_kernel(a_ref, b_ref, g_ref, w_ref, bias_ref, lng_ref, lnb_ref, o_ref, pad_ref, y_ref, *, seg):
    tokens, ch = a_ref.shape
    halo = 2 * SUBLANES
    stride = seg + 2 * halo
    sub = min(seg, CONV_ROWS)
    u = a_ref[...] * _sigmoid(b_ref[...])
    for r in range(tokens // seg):
        base = r * stride
        pad_ref[base:base + halo, :] = jnp.zeros((halo, ch), F32)
        pad_ref[base + halo:base + halo + seg, :] = u[r * seg:(r + 1) * seg]
        pad_ref[base + halo + seg:base + stride, :] = jnp.zeros((halo, ch), F32)
    for r in range(tokens // seg):
        for t0 in range(0, seg, sub):
            base = r * stride + halo - CONV_HALF + t0
            for c0 in range(0, ch, LANES):
                acc = jnp.zeros((sub, LANES), F32) + bias_ref[:, c0:c0 + LANES]
                first = base - (base % SUBLANES)
                rows = sub + 2 * halo
                x_al = pad_ref[first:first + rows, c0:c0 + LANES]
                for sh in range(SUBLANES):
                    off = base - first + sh
                    u_sh = x_al if off == 0 else pltpu.roll(x_al, rows - off, axis=0)
                    for a, j in enumerate(range(sh, CONV_TAPS, SUBLANES)):
                        acc = acc + u_sh[a * SUBLANES:a * SUBLANES + sub] * w_ref[j:j + 1, c0:c0 + LANES]
                y_ref[r * seg + t0:r * seg + t0 + sub, c0:c0 + LANES] = acc
    _cv_post_kernel(y_ref, g_ref, lng_ref, lnb_ref, o_ref)


def _conv_seq(p, w, bias, ln_g, ln_b, seg, tokens=256):
    bsz, l, _ = p.shape
    ch = w.shape[1]
    halo = 2 * SUBLANES
    vec = pl.BlockSpec((1, ch), lambda b, t: (0, 0))
    return pl.pallas_call(
        functools.partial(_conv_seq_kernel, seg=seg),
        grid=(bsz, l // tokens),
        in_specs=[pl.BlockSpec((None, tokens, ch), lambda b, t: (b, t, 5)),
                  pl.BlockSpec((None, tokens, ch), lambda b, t: (b, t, 6)),
                  pl.BlockSpec((None, tokens, ch), lambda b, t: (b, t, 7)),
                  pl.BlockSpec((CONV_TAPS, ch), lambda b, t: (0, 0)), vec, vec, vec],
        out_specs=pl.BlockSpec((None, tokens, ch), lambda b, t: (b, t, 0)),
        out_shape=jax.ShapeDtypeStruct((bsz, l, ch), BF16),
        scratch_shapes=[pltpu.VMEM(((tokens // seg) * (seg + 2 * halo), ch), F32),
                        pltpu.VMEM((tokens, ch), F32)],
        compiler_params=_params(2),
        name="conv_seq",
    )(p, p, p, w, bias.reshape(1, ch), ln_g.reshape(1, ch), ln_b.reshape(1, ch))


def _conv_col_kernel(a_ref, b_ref, w_ref, bias_ref, y_ref, pad_ref, *, width):
    tokens, ch = a_ref.shape
    halo = CONV_HALF * width
    pad_ref[0:halo, :] = jnp.zeros((halo, ch), F32)
    pad_ref[halo:halo + tokens, :] = a_ref[...] * _sigmoid(b_ref[...])
    pad_ref[halo + tokens:halo + tokens + halo, :] = jnp.zeros((halo, ch), F32)

    def row(r, carry):
        t0 = pl.multiple_of(r * width, width)
        for c0 in range(0, ch, LANES):
            acc = jnp.zeros((width, LANES), F32) + bias_ref[:, c0:c0 + LANES]
            for j in range(CONV_TAPS):
                acc = acc + pad_ref[pl.ds(t0 + j * width, width), c0:c0 + LANES] * w_ref[j:j + 1, c0:c0 + LANES]
            y_ref[pl.ds(t0, width), c0:c0 + LANES] = acc
        return carry

    lax.fori_loop(0, tokens // width, row, 0)


def _conv_col(p, w, bias, width, tc=256):
    bsz, l, _ = p.shape
    ch = w.shape[1]
    seg_blocks = ch // tc
    return pl.pallas_call(
        functools.partial(_conv_col_kernel, width=width),
        grid=(bsz, ch // tc),
        in_specs=[pl.BlockSpec((None, l, tc), lambda b, c: (b, 0, 5 * seg_blocks + c)),
                  pl.BlockSpec((None, l, tc), lambda b, c: (b, 0, 6 * seg_blocks + c)),
                  pl.BlockSpec((CONV_TAPS, tc), lambda b, c: (0, c)),
                  pl.BlockSpec((1, tc), lambda b, c: (0, c))],
        out_specs=pl.BlockSpec((None, l, tc), lambda b, c: (b, 0, c)),
        out_shape=jax.ShapeDtypeStruct((bsz, l, ch), F32),
        scratch_shapes=[pltpu.VMEM((l + 2 * CONV_HALF * width, tc), F32)],
        compiler_params=_params(2),
        name="conv_col",
    )(p, p, w, bias.reshape(1, ch))


def _cv_post_kernel(y_ref, g_ref, lg_ref, lb_ref, o_ref):
    y = y_ref[...]
    mu = jnp.mean(y, axis=-1, keepdims=True)
    yc = y - mu
    yn = yc * lax.rsqrt(jnp.mean(yc * yc, axis=-1, keepdims=True) + EPS) * lg_ref[...] + lb_ref[...]
    o_ref[...] = (_silu(yn) * _silu(g_ref[...])).astype(o_ref.dtype)


def _cv_post(y, p, ln_g, ln_b, tl=256):
    b, l, ch = y.shape
    blk = pl.BlockSpec((None, tl, ch), lambda i, j: (i, j, 0))
    vec = pl.BlockSpec((1, ch), lambda i, j: (0, 0))
    return pl.pallas_call(
        _cv_post_kernel,
        grid=(b, l // tl),
        in_specs=[blk, pl.BlockSpec((None, tl, ch), lambda i, j: (i, j, 7)), vec, vec],
        out_specs=blk,
        out_shape=jax.ShapeDtypeStruct((b, l, ch), BF16),
        compiler_params=_params(2),
        name="cv_post",
    )(y, p, ln_g.reshape(1, ch), ln_b.reshape(1, ch))


def _layer(streams, layer, mod, lb_terms, norm_g, w_in, hg_norm_g, w_dw, b_dw, ln_g, ln_b,
           w_hproj, w_cproj, w_out):
    d = streams[0]["x"].shape[-1]
    d_hg = w_hproj.shape[1]
    hs = [_prenorm(s["x"], s["b0"], s["bsz"], norm_g, mod, s["row"]).reshape(-1, d) for s in streams]
    p = _in_proj(hs[0], w_in, layer)
    new_x, finals = [], []
    scans, ps = [], []
    for i, s in enumerate(streams):
        bsz, l = s["bsz"], s["x"].shape[1]
        rider = None
        if i + 1 < len(streams):
            steps = bsz * (d_hg // (SCAN_HEADS_PER_STEP * HEAD_DIM)) * (l // SCAN_CHUNK)
            if _rider_tiles(steps, hs[i + 1].shape[0], w_in.shape[2]) is not None:
                rider = (hs[i + 1], w_in)
        res = list(_hgrn_scan(p.reshape(bsz, l, -1), lb_terms, s["state"], s["state_b0"], layer, s["final"],
                              rider))
        ps.append(p)
        scans.append(res[:2])
        finals.append(res[2] if s["final"] else None)
        if rider is not None:
            p = res[-1]
        elif i + 1 < len(streams):
            p = _in_proj(hs[i + 1], w_in, layer)
    for s, p, (o_f, o_b) in zip(streams, ps, scans):
        bsz, l = s["bsz"], s["x"].shape[1]
        p3 = p.reshape(bsz, l, -1)
        o_hg = _hg_post(o_f, o_b, p3, hg_norm_g)
        if s["conv"] == "col":
            o_cv = _cv_post(_conv_col(p3, w_dw, b_dw, LATENT_GRID_W), p3, ln_g, ln_b)
        else:
            o_cv = _conv_seq(p3, w_dw, b_dw, ln_g, ln_b, LATENT_GRID_W if s["conv"] == "row" else l)
        merged = _merge_proj(o_hg.reshape(bsz * l, d_hg), w_hproj, o_cv.reshape(bsz * l, -1), w_cproj,
                             layer, p, 4 * d)
        x_new = _out_proj(merged, w_out, layer, s["x"].reshape(-1, d), s["b0"] * l, mod, l, s["row"])
        new_x.append(x_new.reshape(bsz, l, d))
    return new_x, finals


def kernel(x_prompt, x_sample, state_hgrn, c, c_ctx, mod_w, mod_b, norm_g, w_in, hg_lb_logits, hg_norm_g,
           cv_dw_w, cv_dw_b, cv_ln_g, cv_ln_b, w_hproj, w_cproj, w_out, final_g):
    depth = mod_w.shape[0]
    d = x_prompt.shape[-1]
    dec_batch = c.shape[0]
    rows = 1 + dec_batch
    pad = (-rows) % SUBLANES
    cond = jnp.concatenate([c_ctx.reshape(1, d), c, jnp.zeros((pad, d), F32)], axis=0).astype(F32)
    mod = _modulation(cond, mod_w, mod_b)
    mod = mod.reshape(depth, rows + pad, 1, 3 * d)
    lb = _lower_bound_terms(hg_lb_logits)
    d_hg = hg_lb_logits.shape[-1]
    lb = lb.reshape(2, depth, 2, d_hg)

    w_in16 = w_in.astype(BF16)
    w_h16 = w_hproj.astype(BF16)
    w_c16 = w_cproj.astype(BF16)
    w_o16 = w_out.astype(BF16)

    parts = 2 if dec_batch % 2 == 0 else 1
    per = dec_batch // parts
    streams = []
    for i in range(parts):
        streams.append(dict(x=x_sample, b0=i * per, bsz=per, row=functools.partial(lambda b, o: 1 + o + b, o=i * per),
                            state=state_hgrn, state_b0=i * per, final=False, conv=None))
    streams.append(dict(x=x_prompt, b0=0, bsz=x_prompt.shape[0], row=lambda b: 0, state=None, state_b0=0,
                        final=True, conv=None))
    states = []
    for l in range(depth):
        lb_l = jnp.transpose(lb[:, l], (1, 0, 2)).reshape(4, d_hg)
        for s in streams[:parts]:
            s["conv"] = "row" if l % 2 == 0 else "col"
        xs, finals = _layer(streams, l, mod[l], lb_l, norm_g[l], w_in16, hg_norm_g[l], cv_dw_w[l], cv_dw_b[l],
                            cv_ln_g[l], cv_ln_b[l], w_h16, w_c16, w_o16)
        for s, x in zip(streams, xs):
            s["x"], s["b0"] = x, 0
        states.append(finals[-1])
    y_prompt = _final_norm(streams[-1]["x"], final_g)
    y_sample = jnp.concatenate([_final_norm(s["x"], final_g) for s in streams[:parts]], axis=0)
    return y_prompt, y_sample, jnp.stack(states, axis=1)
```

```python
import functools

import jax
import jax.numpy as jnp
from jax import lax
from jax.experimental import pallas as pl
from jax.experimental.pallas import tpu as pltpu

F32 = jnp.float32
BF16 = jnp.bfloat16

EPS = 1e-6
LOG2_E = 1.4426950408889634
HEAD_DIM = 128
CONV_TAPS = 31
CONV_HALF = CONV_TAPS // 2
CONV_ROWS = 64
LATENT_GRID_W = 64
SCAN_CHUNK = 256
SCAN_DIAG = 8
SCAN_HEADS_PER_STEP = 2
LANES = 128
SUBLANES = 8
VMEM_LIMIT_BYTES = 56 * 1024 * 1024


def _params(n_axes):
    return pltpu.CompilerParams(dimension_semantics=("arbitrary",) * n_axes,
                                vmem_limit_bytes=VMEM_LIMIT_BYTES)


def _sigmoid(x):
    return 0.5 * jnp.tanh(0.5 * x) + 0.5


def _silu(x):
    return x * _sigmoid(x)


def _lb_kernel(logit_ref, out_ref):
    depth = logit_ref.shape[0]
    x = logit_ref[...]
    mx = jnp.max(x, axis=0, keepdims=True)
    ex = jnp.exp(x - mx)
    p = ex / jnp.sum(ex, axis=0, keepdims=True)
    p0 = p[0:1]
    cum = p0
    for l in range(depth):
        if l > 0:
            cum = cum + p[l:l + 1]
        lb = cum - p0
        out_ref[0, l:l + 1, :] = jnp.log2(lb)
        out_ref[1, l:l + 1, :] = 1.0 - lb


def _lower_bound_terms(lb_logits):
    depth = lb_logits.shape[0]
    flat = lb_logits.reshape(depth, -1).astype(F32)
    return pl.pallas_call(
        _lb_kernel,
        out_shape=jax.ShapeDtypeStruct((2, depth, flat.shape[1]), F32),
        name="lb_terms",
    )(flat)


def _mod_kernel(c_ref, w_ref, b_ref, o_ref):
    a = _silu(c_ref[...]).astype(BF16)
    w = w_ref[...].astype(BF16)
    o_ref[...] = jnp.dot(a, w, preferred_element_type=F32) + b_ref[...]


def _modulation(cond, mod_w, mod_b, tn=512):
    depth, d, n = mod_w.shape
    r = cond.shape[0]
    return pl.pallas_call(
        _mod_kernel,
        grid=(depth, n // tn),
        in_specs=[pl.BlockSpec((r, d), lambda l, j: (0, 0)),
                  pl.BlockSpec((None, d, tn), lambda l, j: (l, 0, j)),
                  pl.BlockSpec((None, 1, tn), lambda l, j: (l, 0, j))],
        out_specs=pl.BlockSpec((None, r, tn), lambda l, j: (l, 0, j)),
        out_shape=jax.ShapeDtypeStruct((depth, r, n), F32),
        compiler_params=_params(2),
        name="adaln_mod",
    )(cond, mod_w, mod_b.reshape(depth, 1, n))


def _prenorm_kernel(x_ref, g_ref, mod_ref, h_ref):
    d = x_ref.shape[-1]
    x = x_ref[...]
    y = x * lax.rsqrt(jnp.mean(x * x, axis=-1, keepdims=True) + EPS) * g_ref[...]
    shift = mod_ref[:, 0:d]
    scale = mod_ref[:, d:2 * d]
    h_ref[...] = (y * (1.0 + scale) + shift).astype(h_ref.dtype)


def _prenorm(x, b0, b, g, mod, row_of_batch, tl=256):
    _, l, d = x.shape
    return pl.pallas_call(
        _prenorm_kernel,
        grid=(b, l // tl),
        in_specs=[pl.BlockSpec((None, tl, d), lambda i, j: (b0 + i, j, 0)),
                  pl.BlockSpec((1, d), lambda i, j: (0, 0)),
                  pl.BlockSpec((None, 1, 3 * d), lambda i, j: (row_of_batch(i), 0, 0))],
        out_specs=pl.BlockSpec((None, tl, d), lambda i, j: (i, j, 0)),
        out_shape=jax.ShapeDtypeStruct((b, l, d), BF16),
        compiler_params=_params(2),
        name="prenorm",
    )(x, g.reshape(1, d), mod)


def _final_norm_kernel(x_ref, g_ref, o_ref):
    x = x_ref[...]
    o_ref[...] = x * lax.rsqrt(jnp.mean(x * x, axis=-1, keepdims=True) + EPS) * g_ref[...]


def _final_norm(x, g, tl=256):
    b, l, d = x.shape
    return pl.pallas_call(
        _final_norm_kernel,
        grid=(b, l // tl),
        in_specs=[pl.BlockSpec((None, tl, d), lambda i, j: (i, j, 0)),
                  pl.BlockSpec((1, d), lambda i, j: (0, 0))],
        out_specs=pl.BlockSpec((None, tl, d), lambda i, j: (i, j, 0)),
        out_shape=jax.ShapeDtypeStruct((b, l, d), F32),
        compiler_params=_params(2),
        name="final_norm",
    )(x, g.reshape(1, d))


def _final_norm_pair_kernel(xa_ref, xb_ref, g_ref, o_ref, *, per):
    i = pl.program_id(0)

    @pl.when(i < per)
    def _():
        _final_norm_kernel(xa_ref, g_ref, o_ref)

    @pl.when(i >= per)
    def _():
        _final_norm_kernel(xb_ref, g_ref, o_ref)


def _final_norm_pair(xa, xb, g, tl=256):
    per, l, d = xa.shape
    nj = l // tl
    return pl.pallas_call(
        functools.partial(_final_norm_pair_kernel, per=per),
        grid=(2 * per, nj),
        in_specs=[pl.BlockSpec((None, tl, d),
                               lambda i, j: (jnp.minimum(i, per - 1), jnp.where(i < per, j, nj - 1), 0)),
                  pl.BlockSpec((None, tl, d),
                               lambda i, j: (jnp.maximum(i - per, 0), jnp.where(i >= per, j, 0), 0)),
                  pl.BlockSpec((1, d), lambda i, j: (0, 0))],
        out_specs=pl.BlockSpec((None, tl, d), lambda i, j: (i, j, 0)),
        out_shape=jax.ShapeDtypeStruct((2 * per, l, d), F32),
        compiler_params=_params(2),
        name="final_norm",
    )(xa, xb, g.reshape(1, d))


def _matmul_kernel(a_ref, w_ref, o_ref):
    o_ref[...] = jnp.dot(a_ref[...], w_ref[...], preferred_element_type=F32).astype(o_ref.dtype)


def _matmul_cast_kernel(a_ref, w_ref, src_ref, o_ref, dst_ref):
    _matmul_kernel(a_ref, w_ref, o_ref)
    dst_ref[...] = src_ref[...].astype(dst_ref.dtype)


def _in_proj(h, w, layer, cast=None, tm=1024, tn=1024):
    m, k = h.shape
    n = w.shape[2]
    tm = min(tm, m)
    n_col = n // tn
    in_specs = [pl.BlockSpec((tm, k), lambda i, j: (i, 0)),
                pl.BlockSpec((None, k, tn), lambda i, j: (layer, 0, j))]
    out_specs = pl.BlockSpec((tm, tn), lambda i, j: (i, j))
    out_shape = jax.ShapeDtypeStruct((m, n), F32)
    if cast is None:
        return pl.pallas_call(
            _matmul_kernel, grid=(m // tm, n_col), in_specs=in_specs, out_specs=out_specs,
            out_shape=out_shape, compiler_params=_params(2), name="in_proj",
        )(h, w)
    src, src_layer = cast
    cw = n // ((m // tm) * n_col)
    in_specs.append(pl.BlockSpec((None, k, cw), lambda i, j: (src_layer, 0, i * n_col + j)))
    return pl.pallas_call(
        _matmul_cast_kernel, grid=(m // tm, n_col), in_specs=in_specs,
        out_specs=[out_specs, pl.BlockSpec((None, k, cw), lambda i, j: (0, 0, i * n_col + j))],
        out_shape=[out_shape, jax.ShapeDtypeStruct((1, k, n), BF16)],
        compiler_params=_params(2), name="in_proj",
    )(h, w, src)


def _cast_slab_ok(m, n, tm=1024, tn=1024):
    steps = (m // min(tm, m)) * (n // tn)
    return n % steps == 0 and (n // steps) % LANES == 0


def _merge_kernel(oh_ref, wh_ref, oc_ref, wc_ref, mh_ref, mc_ref, o_ref):
    yh = jnp.dot(oh_ref[...], wh_ref[...], preferred_element_type=F32)
    yc = jnp.dot(oc_ref[...], wc_ref[...], preferred_element_type=F32)
    o_ref[...] = (_sigmoid(mh_ref[...]) * yh + _sigmoid(mc_ref[...]) * yc).astype(o_ref.dtype)


def _merge_proj(o_hg, w_h, o_cv, w_c, layer, p, gate_col0, tm=1024, tn=512):
    m, kh = o_hg.shape
    kc = o_cv.shape[1]
    n = w_h.shape[2]
    tm = min(tm, m)
    tn = min(tn, n)
    gh = gate_col0 // tn
    gc = (gate_col0 + n) // tn
    return pl.pallas_call(
        _merge_kernel,
        grid=(m // tm, n // tn),
        in_specs=[pl.BlockSpec((tm, kh), lambda i, j: (i, 0)),
                  pl.BlockSpec((None, kh, tn), lambda i, j: (layer, 0, j)),
                  pl.BlockSpec((tm, kc), lambda i, j: (i, 0)),
                  pl.BlockSpec((None, kc, tn), lambda i, j: (layer, 0, j)),
                  pl.BlockSpec((tm, tn), lambda i, j: (i, gh + j)),
                  pl.BlockSpec((tm, tn), lambda i, j: (i, gc + j))],
        out_specs=pl.BlockSpec((tm, tn), lambda i, j: (i, j)),
        out_shape=jax.ShapeDtypeStruct((m, n), BF16),
        compiler_params=_params(2),
        name="merge_proj",
    )(o_hg, w_h, o_cv, w_c, p, p)


def _out_kernel(a_ref, w_ref, x_ref, gate_ref, o_ref):
    y = jnp.dot(a_ref[...], w_ref[...], preferred_element_type=F32)
    o_ref[...] = x_ref[...] + gate_ref[...] * y


def _out_proj(merged, w, layer, x, x_row0, mod, rows_per_batch, row_of_batch, tm=1024, tn=1024):
    m, k = merged.shape
    d = w.shape[2]
    tm = min(tm, m, rows_per_batch)
    tn = min(tn, d)
    blocks_per_batch = rows_per_batch // tm
    gate0 = 2 * d // tn
    x_blk0 = x_row0 // tm
    return pl.pallas_call(
        _out_kernel,
        grid=(m // tm, d // tn),
        in_specs=[pl.BlockSpec((tm, k), lambda i, j: (i, 0)),
                  pl.BlockSpec((None, k, tn), lambda i, j: (layer, 0, j)),
                  pl.BlockSpec((tm, tn), lambda i, j: (x_blk0 + i, j)),
                  pl.BlockSpec((None, 1, tn),
                               lambda i, j: (row_of_batch(i // blocks_per_batch), 0, gate0 + j))],
        out_specs=pl.BlockSpec((tm, tn), lambda i, j: (i, j)),
        out_shape=jax.ShapeDtypeStruct((m, d), F32),
        compiler_params=_params(2),
        name="out_proj",
    )(merged, w, x, mod)


def _dot_nt(a, b):
    return lax.dot_general(a, b, (((1,), (1,)), ((), ())), preferred_element_type=F32)


def _cumsum_rows(x, reverse):
    n = x.shape[0]
    row = lax.broadcasted_iota(jnp.int32, x.shape, 0)
    s = 1
    while s < n:
        if reverse:
            x = x + jnp.where(row < n - s, pltpu.roll(x, n - s, axis=0), 0.0)
        else:
            x = x + jnp.where(row >= s, pltpu.roll(x, s, axis=0), 0.0)
        s *= 2
    return x


def _scan_chunk(q_raw, z, val, la, oml, st_ref, sc_ref, o_ref, q_s, k_s, b_s, v_s, reverse):
    n = q_raw.shape[0]
    q = q_raw * _sigmoid(q_raw)
    z2 = z * LOG2_E
    logf = (jnp.maximum(la, z2) - jnp.maximum(z2, 0.0)
            + jnp.log2(1.0 + jnp.exp2(-jnp.abs(la - z2))) - jnp.log2(1.0 + jnp.exp2(-jnp.abs(z2))))
    k = oml * _sigmoid(-z)
    b = _cumsum_rows(logf, reverse)
    q_s[...] = q
    k_s[...] = k
    b_s[...] = b
    v_s[...] = val
    bound = b_s[0:1, :] if reverse else b_s[n - 1:n, :]

    st = st_ref[...]
    v16 = val.astype(BF16)
    o = _dot_nt((q * jnp.exp2(b)).astype(BF16), st.astype(BF16))
    kd = (k * jnp.exp2(bound - b)).astype(BF16)
    st_ref[...] = st * jnp.exp2(bound) + jnp.dot(val.T.astype(BF16), kd, preferred_element_type=F32)

    lane = lax.broadcasted_iota(jnp.int32, (SUBLANES, LANES), 1)
    m = n // 2
    while m >= SCAN_DIAG:
        groups = n // (2 * m)
        t_off = 0 if reverse else m
        s_off = m if reverse else 0
        r_off = m - 1 if reverse else m
        args = []
        for g in range(groups):
            lo = b_s[g * 2 * m:g * 2 * m + m, :]
            hi = b_s[g * 2 * m + m:(g + 1) * 2 * m, :]
            ref = b_s[g * 2 * m + r_off:g * 2 * m + r_off + 1, :]
            args += [lo - ref, ref - hi] if reverse else [ref - lo, hi - ref]
        ew = jnp.exp2(jnp.concatenate(args, axis=0))
        qe = q_s[...] * ew
        ke = (k_s[...] * ew).astype(BF16)
        qt = jnp.concatenate([qe[g * 2 * m + t_off:g * 2 * m + t_off + m] for g in range(groups)],
                             axis=0).astype(BF16)
        g_all = _dot_nt(qt, ke)
        for g in range(groups):
            t0 = g * 2 * m + t_off
            s0 = g * 2 * m + s_off
            tile0 = (s0 // LANES) * LANES
            if m >= LANES:
                sc_ref[t0:t0 + m, s0:s0 + m] = g_all[g * m:(g + 1) * m, s0:s0 + m]
            else:
                keep = (lane >= s0 - tile0) & (lane < s0 - tile0 + m)
                for i in range(m // SUBLANES):
                    rows = slice(t0 + i * SUBLANES, t0 + (i + 1) * SUBLANES)
                    new = g_all[g * m + i * SUBLANES:g * m + (i + 1) * SUBLANES, tile0:tile0 + LANES]
                    sc_ref[rows, tile0:tile0 + LANES] = jnp.where(
                        keep, new, sc_ref[rows, tile0:tile0 + LANES])
        m //= 2

    o = o + jnp.dot(sc_ref[...].astype(BF16), v16, preferred_element_type=F32)

    trow = lax.broadcasted_iota(jnp.int32, (SCAN_DIAG, LANES), 0)
    diag = []
    for t0 in range(0, n, SCAN_DIAG):
        qb = q_s[t0:t0 + SCAN_DIAG, :]
        bb = b_s[t0:t0 + SCAN_DIAG, :]
        acc = None
        for s in range(SCAN_DIAG):
            ks = k_s[t0 + s:t0 + s + 1, :]
            bs = b_s[t0 + s:t0 + s + 1, :]
            vs = v_s[t0 + s:t0 + s + 1, :]
            w = qb * ks * jnp.exp2(bb - bs)
            if 0 < s < SCAN_DIAG - 1 or (s == 0 and reverse) or (s == SCAN_DIAG - 1 and not reverse):
                w = jnp.where((trow <= s) if reverse else (trow >= s), w, 0.0)
            term = jnp.sum(w, axis=-1, keepdims=True) * vs
            acc = term if acc is None else acc + term
        diag.append(acc)
    o_ref[...] = o + jnp.concatenate(diag, axis=0)


def _scan_kernel(*refs, n_chunks, hb, has_init, want_final, has_mm):
    lb_ref, qf_ref, vf_ref, zf_ref, qb_ref, vb_ref, zb_ref = refs[:7]
    pos = 7
    s0_ref = None
    if has_init:
        s0_ref = refs[pos]
        pos += 1
    if has_mm:
        a_ref, w_ref = refs[pos:pos + 2]
        pos += 2
    of_ref, ob_ref = refs[pos:pos + 2]
    pos += 2
    sfin_ref = None
    if want_final:
        sfin_ref = refs[pos]
        pos += 1
    if has_mm:
        mm_ref = refs[pos]
        pos += 1
    st_ref, sc_ref, tmp_ref = refs[pos:]
    c = pl.program_id(2)

    @pl.when(c == 0)
    def _():
        sc_ref[...] = jnp.zeros(sc_ref.shape, F32)
        for d in range(2):
            for h in range(hb):
                if has_init:
                    st_ref[d, h] = s0_ref[d, h].T
                else:
                    st_ref[d, h] = jnp.zeros((HEAD_DIM, HEAD_DIM), F32)

    for h in range(hb):
        lanes = slice(h * HEAD_DIM, (h + 1) * HEAD_DIM)
        for d, (q_ref, v_ref, z_ref, o_ref) in enumerate(
                ((qf_ref, vf_ref, zf_ref, of_ref), (qb_ref, vb_ref, zb_ref, ob_ref))):
            if has_mm:
                part = a_ref.shape[0] // (2 * hb)
                rows = slice((2 * h + d) * part, (2 * h + d + 1) * part)
                mm_ref[rows, :] = jnp.dot(a_ref[rows, :], w_ref[...], preferred_element_type=F32)
            _scan_chunk(q_ref[:, lanes], z_ref[:, lanes], v_ref[:, lanes],
                        lb_ref[2 * d:2 * d + 1, lanes], lb_ref[2 * d + 1:2 * d + 2, lanes],
                        st_ref.at[d, h], sc_ref.at[d, h], o_ref.at[:, lanes],
                        *(tmp_ref.at[d, h, i] for i in range(4)), reverse=(d == 1))

    if want_final:
        @pl.when(c == n_chunks - 1)
        def _():
            for d in range(2):
                for h in range(hb):
                    sfin_ref[d, h] = st_ref[d, h].T


def _rider_tiles(steps, m, n):
    for tn in (768, 1024, 512, 1536, 2048, 3072, 256, 384):
        if n % tn:
            continue
        n_col = n // tn
        if steps % n_col or m % (steps // n_col):
            continue
        tm = m // (steps // n_col)
        if tm % (2 * SUBLANES) == 0 and tm <= 1024:
            return tm, tn
    return None


def _hgrn_scan(p, lb_terms, state, state_b0, layer, want_final, rider=None):
    bsz, l, _ = p.shape
    d_hg = lb_terms.shape[1]
    heads = d_hg // HEAD_DIM
    hb = SCAN_HEADS_PER_STEP
    w = hb * HEAD_DIM
    n_chunks = l // SCAN_CHUNK
    n_groups = heads // hb
    seg = d_hg // w
    has_init = state is not None

    def fwd(col):
        return pl.BlockSpec((None, SCAN_CHUNK, w), lambda b, g, c: (b, c, col * seg + g))

    def bwd(col):
        return pl.BlockSpec((None, SCAN_CHUNK, w), lambda b, g, c: (b, n_chunks - 1 - c, col * seg + g))

    in_specs = [pl.BlockSpec((4, w), lambda b, g, c: (0, g)),
                fwd(0), fwd(3), fwd(1), bwd(0), bwd(3), bwd(2)]
    args = [lb_terms, p, p, p, p, p, p]
    if has_init:
        in_specs.append(pl.BlockSpec((None, None, 2, hb, HEAD_DIM, HEAD_DIM),
                                     lambda b, g, c: (state_b0 + b, layer, 0, g, 0, 0)))
        args.append(state)
    out_specs = [pl.BlockSpec((None, SCAN_CHUNK, w), lambda b, g, c: (b, c, g)),
                 pl.BlockSpec((None, SCAN_CHUNK, w), lambda b, g, c: (b, n_chunks - 1 - c, g))]
    out_shape = [jax.ShapeDtypeStruct((bsz, l, d_hg), F32)] * 2
    if want_final:
        out_specs.append(pl.BlockSpec((None, 2, hb, HEAD_DIM, HEAD_DIM), lambda b, g, c: (b, 0, g, 0, 0)))
        out_shape.append(jax.ShapeDtypeStruct((bsz, 2, heads, HEAD_DIM, HEAD_DIM), F32))
    if rider is not None:
        a, wr, wl = rider
        m, k = a.shape
        n = wr.shape[2]
        tm, tn = _rider_tiles(bsz * n_groups * n_chunks, m, n)
        n_col = n // tn

        def step(b, g, c):
            return (b * n_groups + g) * n_chunks + c

        in_specs += [pl.BlockSpec((tm, k), lambda b, g, c: (step(b, g, c) // n_col, 0)),
                     pl.BlockSpec((None, k, tn), lambda b, g, c: (wl, 0, step(b, g, c) % n_col))]
        args += [a, wr]
        out_specs.append(pl.BlockSpec((tm, tn), lambda b, g, c: (step(b, g, c) // n_col, step(b, g, c) % n_col)))
        out_shape.append(jax.ShapeDtypeStruct((m, n), F32))
    scratch = [pltpu.VMEM((2, hb, HEAD_DIM, HEAD_DIM), F32),
               pltpu.VMEM((2, hb, SCAN_CHUNK, SCAN_CHUNK), F32),
               pltpu.VMEM((2, hb, 4, SCAN_CHUNK, HEAD_DIM), F32)]
    return pl.pallas_call(
        functools.partial(_scan_kernel, n_chunks=n_chunks, hb=hb, has_init=has_init, want_final=want_final,
                          has_mm=rider is not None),
        grid=(bsz, n_groups, n_chunks),
        in_specs=in_specs,
        out_specs=out_specs,
        out_shape=out_shape,
        scratch_shapes=scratch,
        compiler_params=_params(3),
        name="hgrn_scan",
    )(*args)


def _hg_post_kernel(of_ref, ob_ref, g_ref, ng_ref, o_ref):
    heads = of_ref.shape[-1] // HEAD_DIM
    for h in range(heads):
        lanes = slice(h * HEAD_DIM, (h + 1) * HEAD_DIM)
        o = of_ref[:, lanes] + ob_ref[:, lanes]
        o = o * lax.rsqrt(jnp.mean(o * o, axis=-1, keepdims=True) + EPS) * ng_ref[...]
        o_ref[:, lanes] = (o * _silu(g_ref[:, lanes])).astype(o_ref.dtype)


def _hg_post(o_f, o_b, p, norm_g, tl=256):
    b, l, d_hg = o_f.shape
    blk = pl.BlockSpec((None, tl, d_hg), lambda i, j: (i, j, 0))
    return pl.pallas_call(
        _hg_post_kernel,
        grid=(b, l // tl),
        in_specs=[blk, blk,
                  pl.BlockSpec((None, tl, d_hg), lambda i, j: (i, j, 4)),
                  pl.BlockSpec((1, HEAD_DIM), lambda i, j: (0, 0))],
        out_specs=blk,
        out_shape=jax.ShapeDtypeStruct((b, l, d_hg), BF16),
        compiler_params=_params(2),
        name="hg_post",
    )(o_f, o_b, p, norm_g.reshape(1, HEAD_DIM))


def _conv_seq_kernel(a_ref, b_ref, g_ref, w_ref, bias_ref, lng_ref, lnb_ref, o_ref, pad_ref, y_ref, *, seg):
    tokens, ch = a_ref.shape
    halo = 2 * SUBLANES
    stride = seg + 2 * halo
    sub = min(seg, CONV_ROWS)
    u = a_ref[...] * _sigmoid(b_ref[...])
    for r in range(tokens // seg):
        base = r * stride
        pad_ref[base:base + halo, :] = jnp.zeros((halo, ch), F32)
        pad_ref[base + halo:base + halo + seg, :] = u[r * seg:(r + 1) * seg]
        pad_ref[base + halo + seg:base + stride, :] = jnp.zeros((halo, ch), F32)
    for r in range(tokens // seg):
        for t0 in range(0, seg, sub):
            base = r * stride + halo - CONV_HALF + t0
            for c0 in range(0, ch, LANES):
                acc = jnp.zeros((sub, LANES), F32) + bias_ref[:, c0:c0 + LANES]
                first = base - (base % SUBLANES)
                rows = sub + 2 * halo
                x_al = pad_ref[first:first + rows, c0:c0 + LANES]
                for sh in range(SUBLANES):
                    off = base - first + sh
                    u_sh = x_al if off == 0 else pltpu.roll(x_al, rows - off, axis=0)
                    for a, j in enumerate(range(sh, CONV_TAPS, SUBLANES)):
                        acc = acc + u_sh[a * SUBLANES:a * SUBLANES + sub] * w_ref[j:j + 1, c0:c0 + LANES]
                y_ref[r * seg + t0:r * seg + t0 + sub, c0:c0 + LANES] = acc
    _cv_post_kernel(y_ref, g_ref, lng_ref, lnb_ref, o_ref)


def _conv_seq(p, w, bias, ln_g, ln_b, seg, tokens=256):
    bsz, l, _ = p.shape
    ch = w.shape[1]
    halo = 2 * SUBLANES
    vec = pl.BlockSpec((1, ch), lambda b, t: (0, 0))
    return pl.pallas_call(
        functools.partial(_conv_seq_kernel, seg=seg),
        grid=(bsz, l // tokens),
        in_specs=[pl.BlockSpec((None, tokens, ch), lambda b, t: (b, t, 5)),
                  pl.BlockSpec((None, tokens, ch), lambda b, t: (b, t, 6)),
                  pl.BlockSpec((None, tokens, ch), lambda b, t: (b, t, 7)),
                  pl.BlockSpec((CONV_TAPS, ch), lambda b, t: (0, 0)), vec, vec, vec],
        out_specs=pl.BlockSpec((None, tokens, ch), lambda b, t: (b, t, 0)),
        out_shape=jax.ShapeDtypeStruct((bsz, l, ch), BF16),
        scratch_shapes=[pltpu.VMEM(((tokens // seg) * (seg + 2 * halo), ch), F32),
                        pltpu.VMEM((tokens, ch), F32)],
        compiler_params=_params(2),
        name="conv_seq",
    )(p, p, p, w, bias.reshape(1, ch), ln_g.reshape(1, ch), ln_b.reshape(1, ch))


def _conv_col_kernel(a_ref, b_ref, w_ref, bias_ref, y_ref, pad_ref, *, width):
    tokens, ch = a_ref.shape
    halo = CONV_HALF * width
    pad_ref[0:halo, :] = jnp.zeros((halo, ch), F32)
    pad_ref[halo:halo + tokens, :] = a_ref[...] * _sigmoid(b_ref[...])
    pad_ref[halo + tokens:halo + tokens + halo, :] = jnp.zeros((halo, ch), F32)

    def row(r, carry):
        t0 = pl.multiple_of(r * width, width)
        for c0 in range(0, ch, LANES):
            acc = jnp.zeros((width, LANES), F32) + bias_ref[:, c0:c0 + LANES]
            for j in range(CONV_TAPS):
                acc = acc + pad_ref[pl.ds(t0 + j * width, width), c0:c0 + LANES] * w_ref[j:j + 1, c0:c0 + LANES]
            y_ref[pl.ds(t0, width), c0:c0 + LANES] = acc
        return carry

    lax.fori_loop(0, tokens // width, row, 0)


def _conv_col(p, w, bias, width, tc=256):
    bsz, l, _ = p.shape
    ch = w.shape[1]
    seg_blocks = ch // tc
    return pl.pallas_call(
        functools.partial(_conv_col_kernel, width=width),
        grid=(bsz, ch // tc),
        in_specs=[pl.BlockSpec((None, l, tc), lambda b, c: (b, 0, 5 * seg_blocks + c)),
                  pl.BlockSpec((None, l, tc), lambda b, c: (b, 0, 6 * seg_blocks + c)),
                  pl.BlockSpec((CONV_TAPS, tc), lambda b, c: (0, c)),
                  pl.BlockSpec((1, tc), lambda b, c: (0, c))],
        out_specs=pl.BlockSpec((None, l, tc), lambda b, c: (b, 0, c)),
        out_shape=jax.ShapeDtypeStruct((bsz, l, ch), F32),
        scratch_shapes=[pltpu.VMEM((l + 2 * CONV_HALF * width, tc), F32)],
        compiler_params=_params(2),
        name="conv_col",
    )(p, p, w, bias.reshape(1, ch))


def _cv_post_kernel(y_ref, g_ref, lg_ref, lb_ref, o_ref):
    y = y_ref[...]
    mu = jnp.mean(y, axis=-1, keepdims=True)
    yc = y - mu
    yn = yc * lax.rsqrt(jnp.mean(yc * yc, axis=-1, keepdims=True) + EPS) * lg_ref[...] + lb_ref[...]
    o_ref[...] = (_silu(yn) * _silu(g_ref[...])).astype(o_ref.dtype)


def _cv_post(y, p, ln_g, ln_b, tl=256):
    b, l, ch = y.shape
    blk = pl.BlockSpec((None, tl, ch), lambda i, j: (i, j, 0))
    vec = pl.BlockSpec((1, ch), lambda i, j: (0, 0))
    return pl.pallas_call(
        _cv_post_kernel,
        grid=(b, l // tl),
        in_specs=[blk, pl.BlockSpec((None, tl, ch), lambda i, j: (i, j, 7)), vec, vec],
        out_specs=blk,
        out_shape=jax.ShapeDtypeStruct((b, l, ch), BF16),
        compiler_params=_params(2),
        name="cv_post",
    )(y, p, ln_g.reshape(1, ch), ln_b.reshape(1, ch))


def _layer(streams, layer, mod, lb_terms, norm_g, w_in, w_in_layer, w_in_cast, hg_norm_g, w_dw, b_dw,
           ln_g, ln_b, w_hproj, w_cproj, w_out):
    d = streams[0]["x"].shape[-1]
    d_hg = w_hproj.shape[1]
    hs = [_prenorm(s["x"], s["b0"], s["bsz"], norm_g, mod, s["row"]).reshape(-1, d) for s in streams]
    w_next = None
    if w_in_cast is not None and _cast_slab_ok(hs[0].shape[0], w_in.shape[2]):
        p, w_next = _in_proj(hs[0], w_in, w_in_layer, w_in_cast)
    else:
        p = _in_proj(hs[0], w_in, w_in_layer)
        if w_in_cast is not None:
            w_next = w_in_cast[0][w_in_cast[1]:w_in_cast[1] + 1].astype(BF16)
    new_x, finals = [], []
    scans, ps = [], []
    for i, s in enumerate(streams):
        bsz, l = s["bsz"], s["x"].shape[1]
        rider = None
        if i + 1 < len(streams):
            steps = bsz * (d_hg // (SCAN_HEADS_PER_STEP * HEAD_DIM)) * (l // SCAN_CHUNK)
            if _rider_tiles(steps, hs[i + 1].shape[0], w_in.shape[2]) is not None:
                rider = (hs[i + 1], w_in, w_in_layer)
        res = list(_hgrn_scan(p.reshape(bsz, l, -1), lb_terms, s["state"], s["state_b0"], layer, s["final"],
                              rider))
        ps.append(p)
        scans.append(res[:2])
        finals.append(res[2] if s["final"] else None)
        if rider is not None:
            p = res[-1]
        elif i + 1 < len(streams):
            p = _in_proj(hs[i + 1], w_in, w_in_layer)
    for s, p, (o_f, o_b) in zip(streams, ps, scans):
        bsz, l = s["bsz"], s["x"].shape[1]
        p3 = p.reshape(bsz, l, -1)
        o_hg = _hg_post(o_f, o_b, p3, hg_norm_g)
        if s["conv"] == "col":
            o_cv = _cv_post(_conv_col(p3, w_dw, b_dw, LATENT_GRID_W), p3, ln_g, ln_b)
        else:
            o_cv = _conv_seq(p3, w_dw, b_dw, ln_g, ln_b, LATENT_GRID_W if s["conv"] == "row" else l)
        merged = _merge_proj(o_hg.reshape(bsz * l, d_hg), w_hproj, o_cv.reshape(bsz * l, -1), w_cproj,
                             layer, p, 4 * d)
        x_new = _out_proj(merged, w_out, layer, s["x"].reshape(-1, d), s["b0"] * l, mod, l, s["row"])
        new_x.append(x_new.reshape(bsz, l, d))
    return new_x, finals, w_next


def kernel(x_prompt, x_sample, state_hgrn, c, c_ctx, mod_w, mod_b, norm_g, w_in, hg_lb_logits, hg_norm_g,
           cv_dw_w, cv_dw_b, cv_ln_g, cv_ln_b, w_hproj, w_cproj, w_out, final_g):
    depth = mod_w.shape[0]
    d = x_prompt.shape[-1]
    dec_batch = c.shape[0]
    rows = 1 + dec_batch
    pad = (-rows) % SUBLANES
    cond = jnp.concatenate([c_ctx.reshape(1, d), c, jnp.zeros((pad, d), F32)], axis=0).astype(F32)
    mod = _modulation(cond, mod_w, mod_b)
    mod = mod.reshape(depth, rows + pad, 1, 3 * d)
    lb = _lower_bound_terms(hg_lb_logits)
    d_hg = hg_lb_logits.shape[-1]
    lb = lb.reshape(2, depth, 2, d_hg)

    w_in16 = w_in[0:1].astype(BF16)
    w_h16 = w_hproj.astype(BF16)
    w_c16 = w_cproj.astype(BF16)
    w_o16 = w_out.astype(BF16)

    parts = 2 if dec_batch % 2 == 0 else 1
    per = dec_batch // parts
    streams = []
    for i in range(parts):
        streams.append(dict(x=x_sample, b0=i * per, bsz=per, row=functools.partial(lambda b, o: 1 + o + b, o=i * per),
                            state=state_hgrn, state_b0=i * per, final=False, conv=None))
    streams.append(dict(x=x_prompt, b0=0, bsz=x_prompt.shape[0], row=lambda b: 0, state=None, state_b0=0,
                        final=True, conv=None))
    states = []
    for l in range(depth):
        lb_l = jnp.transpose(lb[:, l], (1, 0, 2)).reshape(4, d_hg)
        for s in streams[:parts]:
            s["conv"] = "row" if l % 2 == 0 else "col"
        xs, finals, w_next = _layer(streams, l, mod[l], lb_l, norm_g[l], w_in16, 0,
                                    (w_in, l + 1) if l + 1 < depth else None, hg_norm_g[l], cv_dw_w[l],
                                    cv_dw_b[l], cv_ln_g[l], cv_ln_b[l], w_h16, w_c16, w_o16)
        w_in16 = w_next
        for s, x in zip(streams, xs):
            s["x"], s["b0"] = x, 0
        states.append(finals[-1])
    y_prompt = _final_norm(streams[-1]["x"], final_g)
    if parts == 2:
        y_sample = _final_norm_pair(streams[0]["x"], streams[1]["x"], final_g)
    else:
        y_sample = _final_norm(streams[0]["x"], final_g)
    return y_prompt, y_sample, jnp.stack(states, axis=1)
```

```python
import functools

import jax
import jax.numpy as jnp
from jax import lax
from jax.experimental import pallas as pl
from jax.experimental.pallas import tpu as pltpu

F32 = jnp.float32
BF16 = jnp.bfloat16

EPS = 1e-6
LOG2_E = 1.4426950408889634
HEAD_DIM = 128
CONV_TAPS = 31
CONV_HALF = CONV_TAPS // 2
CONV_ROWS = 64
LATENT_GRID_W = 64
SCAN_CHUNK = 256
SCAN_DIAG = 8
SCAN_HEADS_PER_STEP = 2
LANES = 128
SUBLANES = 8
VMEM_LIMIT_BYTES = 56 * 1024 * 1024


def _params(n_axes):
    return pltpu.CompilerParams(dimension_semantics=("arbitrary",) * n_axes,
                                vmem_limit_bytes=VMEM_LIMIT_BYTES)


def _sigmoid(x):
    return 0.5 * jnp.tanh(0.5 * x) + 0.5


def _silu(x):
    return x * _sigmoid(x)


def _lb_kernel(logit_ref, out_ref):
    depth = logit_ref.shape[0]
    x = logit_ref[...]
    mx = jnp.max(x, axis=0, keepdims=True)
    ex = jnp.exp(x - mx)
    p = ex / jnp.sum(ex, axis=0, keepdims=True)
    p0 = p[0:1]
    cum = p0
    for l in range(depth):
        if l > 0:
            cum = cum + p[l:l + 1]
        lb = cum - p0
        out_ref[0, l:l + 1, :] = jnp.log2(lb)
        out_ref[1, l:l + 1, :] = 1.0 - lb


def _lower_bound_terms(lb_logits):
    depth = lb_logits.shape[0]
    flat = lb_logits.reshape(depth, -1).astype(F32)
    return pl.pallas_call(
        _lb_kernel,
        out_shape=jax.ShapeDtypeStruct((2, depth, flat.shape[1]), F32),
        name="lb_terms",
    )(flat)


def _mod_kernel(c_ref, w_ref, b_ref, o_ref):
    a = _silu(c_ref[...]).astype(BF16)
    w = w_ref[...].astype(BF16)
    o_ref[...] = jnp.dot(a, w, preferred_element_type=F32) + b_ref[...]


def _modulation(cond, mod_w, mod_b, tn=512):
    depth, d, n = mod_w.shape
    r = cond.shape[0]
    return pl.pallas_call(
        _mod_kernel,
        grid=(depth, n // tn),
        in_specs=[pl.BlockSpec((r, d), lambda l, j: (0, 0)),
                  pl.BlockSpec((None, d, tn), lambda l, j: (l, 0, j)),
                  pl.BlockSpec((None, 1, tn), lambda l, j: (l, 0, j))],
        out_specs=pl.BlockSpec((None, r, tn), lambda l, j: (l, 0, j)),
        out_shape=jax.ShapeDtypeStruct((depth, r, n), F32),
        compiler_params=_params(2),
        name="adaln_mod",
    )(cond, mod_w, mod_b.reshape(depth, 1, n))


def _prenorm_kernel(x_ref, g_ref, mod_ref, h_ref):
    d = x_ref.shape[-1]
    x = x_ref[...]
    y = x * lax.rsqrt(jnp.mean(x * x, axis=-1, keepdims=True) + EPS) * g_ref[...]
    shift = mod_ref[:, 0:d]
    scale = mod_ref[:, d:2 * d]
    h_ref[...] = (y * (1.0 + scale) + shift).astype(h_ref.dtype)


def _prenorm(x, b0, b, g, mod, row_of_batch, tl=256):
    _, l, d = x.shape
    return pl.pallas_call(
        _prenorm_kernel,
        grid=(b, l // tl),
        in_specs=[pl.BlockSpec((None, tl, d), lambda i, j: (b0 + i, j, 0)),
                  pl.BlockSpec((1, d), lambda i, j: (0, 0)),
                  pl.BlockSpec((None, 1, 3 * d), lambda i, j: (row_of_batch(i), 0, 0))],
        out_specs=pl.BlockSpec((None, tl, d), lambda i, j: (i, j, 0)),
        out_shape=jax.ShapeDtypeStruct((b, l, d), BF16),
        compiler_params=_params(2),
        name="prenorm",
    )(x, g.reshape(1, d), mod)


def _final_norm_kernel(x_ref, g_ref, o_ref):
    x = x_ref[...]
    o_ref[...] = x * lax.rsqrt(jnp.mean(x * x, axis=-1, keepdims=True) + EPS) * g_ref[...]


def _final_norm(x, g, tl=256):
    b, l, d = x.shape
    return pl.pallas_call(
        _final_norm_kernel,
        grid=(b, l // tl),
        in_specs=[pl.BlockSpec((None, tl, d), lambda i, j: (i, j, 0)),
                  pl.BlockSpec((1, d), lambda i, j: (0, 0))],
        out_specs=pl.BlockSpec((None, tl, d), lambda i, j: (i, j, 0)),
        out_shape=jax.ShapeDtypeStruct((b, l, d), F32),
        compiler_params=_params(2),
        name="final_norm",
    )(x, g.reshape(1, d))


def _final_norm_pair_kernel(xa_ref, xb_ref, g_ref, o_ref, *, per):
    i = pl.program_id(0)

    @pl.when(i < per)
    def _():
        _final_norm_kernel(xa_ref, g_ref, o_ref)

    @pl.when(i >= per)
    def _():
        _final_norm_kernel(xb_ref, g_ref, o_ref)


def _final_norm_pair(xa, xb, g, tl=256):
    per, l, d = xa.shape
    nj = l // tl
    return pl.pallas_call(
        functools.partial(_final_norm_pair_kernel, per=per),
        grid=(2 * per, nj),
        in_specs=[pl.BlockSpec((None, tl, d),
                               lambda i, j: (jnp.minimum(i, per - 1), jnp.where(i < per, j, nj - 1), 0)),
                  pl.BlockSpec((None, tl, d),
                               lambda i, j: (jnp.maximum(i - per, 0), jnp.where(i >= per, j, 0), 0)),
                  pl.BlockSpec((1, d), lambda i, j: (0, 0))],
        out_specs=pl.BlockSpec((None, tl, d), lambda i, j: (i, j, 0)),
        out_shape=jax.ShapeDtypeStruct((2 * per, l, d), F32),
        compiler_params=_params(2),
        name="final_norm",
    )(xa, xb, g.reshape(1, d))


def _matmul_kernel(a_ref, w_ref, o_ref):
    o_ref[...] = jnp.dot(a_ref[...], w_ref[...], preferred_element_type=F32).astype(o_ref.dtype)


def _matmul_cast_kernel(a_ref, w_ref, src_ref, o_ref, dst_ref):
    _matmul_kernel(a_ref, w_ref, o_ref)
    dst_ref[...] = src_ref[...].astype(dst_ref.dtype)


def _in_proj(h, w, layer, cast=None, tm=1024, tn=1024):
    m, k = h.shape
    n = w.shape[2]
    tm = min(tm, m)
    n_col = n // tn
    in_specs = [pl.BlockSpec((tm, k), lambda i, j: (i, 0)),
                pl.BlockSpec((None, k, tn), lambda i, j: (layer, 0, j))]
    out_specs = pl.BlockSpec((tm, tn), lambda i, j: (i, j))
    out_shape = jax.ShapeDtypeStruct((m, n), F32)
    if cast is None:
        return pl.pallas_call(
            _matmul_kernel, grid=(m // tm, n_col), in_specs=in_specs, out_specs=out_specs,
            out_shape=out_shape, compiler_params=_params(2), name="in_proj",
        )(h, w)
    src, src_layer = cast
    cw = n // ((m // tm) * n_col)
    in_specs.append(pl.BlockSpec((None, k, cw), lambda i, j: (src_layer, 0, i * n_col + j)))
    return pl.pallas_call(
        _matmul_cast_kernel, grid=(m // tm, n_col), in_specs=in_specs,
        out_specs=[out_specs, pl.BlockSpec((None, k, cw), lambda i, j: (0, 0, i * n_col + j))],
        out_shape=[out_shape, jax.ShapeDtypeStruct((1, k, n), BF16)],
        compiler_params=_params(2), name="in_proj",
    )(h, w, src)


def _cast_slab_ok(m, n, tm=1024, tn=1024):
    steps = (m // min(tm, m)) * (n // tn)
    return n % steps == 0 and (n // steps) % LANES == 0


def _cast_kernel(src_ref, dst_ref):
    dst_ref[...] = src_ref[...].astype(dst_ref.dtype)


def _cast_layer(w, layer, tn=512):
    _, k, n = w.shape
    return pl.pallas_call(
        _cast_kernel,
        grid=(n // tn,),
        in_specs=[pl.BlockSpec((None, k, tn), lambda j: (layer, 0, j))],
        out_specs=pl.BlockSpec((None, k, tn), lambda j: (0, 0, j)),
        out_shape=jax.ShapeDtypeStruct((1, k, n), BF16),
        compiler_params=_params(1),
        name="cast_layer",
    )(w)


def _merge_kernel(oh_ref, wh_ref, oc_ref, wc_ref, mh_ref, mc_ref, o_ref):
    yh = jnp.dot(oh_ref[...], wh_ref[...], preferred_element_type=F32)
    yc = jnp.dot(oc_ref[...], wc_ref[...], preferred_element_type=F32)
    o_ref[...] = (_sigmoid(mh_ref[...]) * yh + _sigmoid(mc_ref[...]) * yc).astype(o_ref.dtype)


def _merge_proj(o_hg, w_h, o_cv, w_c, layer, p, gate_col0, tm=1024, tn=512):
    m, kh = o_hg.shape
    kc = o_cv.shape[1]
    n = w_h.shape[2]
    tm = min(tm, m)
    tn = min(tn, n)
    gh = gate_col0 // tn
    gc = (gate_col0 + n) // tn
    return pl.pallas_call(
        _merge_kernel,
        grid=(m // tm, n // tn),
        in_specs=[pl.BlockSpec((tm, kh), lambda i, j: (i, 0)),
                  pl.BlockSpec((None, kh, tn), lambda i, j: (layer, 0, j)),
                  pl.BlockSpec((tm, kc), lambda i, j: (i, 0)),
                  pl.BlockSpec((None, kc, tn), lambda i, j: (layer, 0, j)),
                  pl.BlockSpec((tm, tn), lambda i, j: (i, gh + j)),
                  pl.BlockSpec((tm, tn), lambda i, j: (i, gc + j))],
        out_specs=pl.BlockSpec((tm, tn), lambda i, j: (i, j)),
        out_shape=jax.ShapeDtypeStruct((m, n), BF16),
        compiler_params=_params(2),
        name="merge_proj",
    )(o_hg, w_h, o_cv, w_c, p, p)


def _out_kernel(a_ref, w_ref, x_ref, gate_ref, o_ref):
    y = jnp.dot(a_ref[...], w_ref[...], preferred_element_type=F32)
    o_ref[...] = x_ref[...] + gate_ref[...] * y


def _out_proj(merged, w, layer, x, x_row0, mod, rows_per_batch, row_of_batch, tm=1024, tn=1024):
    m, k = merged.shape
    d = w.shape[2]
    tm = min(tm, m, rows_per_batch)
    tn = min(tn, d)
    blocks_per_batch = rows_per_batch // tm
    gate0 = 2 * d // tn
    x_blk0 = x_row0 // tm
    return pl.pallas_call(
        _out_kernel,
        grid=(m // tm, d // tn),
        in_specs=[pl.BlockSpec((tm, k), lambda i, j: (i, 0)),
                  pl.BlockSpec((None, k, tn), lambda i, j: (layer, 0, j)),
                  pl.BlockSpec((tm, tn), lambda i, j: (x_blk0 + i, j)),
                  pl.BlockSpec((None, 1, tn),
                               lambda i, j: (row_of_batch(i // blocks_per_batch), 0, gate0 + j))],
        out_specs=pl.BlockSpec((tm, tn), lambda i, j: (i, j)),
        out_shape=jax.ShapeDtypeStruct((m, d), F32),
        compiler_params=_params(2),
        name="out_proj",
    )(merged, w, x, mod)


def _dot_nt(a, b):
    return lax.dot_general(a, b, (((1,), (1,)), ((), ())), preferred_element_type=F32)


def _cumsum_rows(x, reverse):
    n = x.shape[0]
    row = lax.broadcasted_iota(jnp.int32, x.shape, 0)
    s = 1
    while s < n:
        if reverse:
            x = x + jnp.where(row < n - s, pltpu.roll(x, n - s, axis=0), 0.0)
        else:
            x = x + jnp.where(row >= s, pltpu.roll(x, s, axis=0), 0.0)
        s *= 2
    return x


def _scan_chunk(q_raw, z, val, la, oml, st_ref, sc_ref, o_ref, q_s, k_s, b_s, v_s, reverse):
    n = q_raw.shape[0]
    q = q_raw * _sigmoid(q_raw)
    z2 = z * LOG2_E
    logf = (jnp.maximum(la, z2) - jnp.maximum(z2, 0.0)
            + jnp.log2(1.0 + jnp.exp2(-jnp.abs(la - z2))) - jnp.log2(1.0 + jnp.exp2(-jnp.abs(z2))))
    k = oml * _sigmoid(-z)
    b = _cumsum_rows(logf, reverse)
    q_s[...] = q
    k_s[...] = k
    b_s[...] = b
    v_s[...] = val
    bound = b_s[0:1, :] if reverse else b_s[n - 1:n, :]

    st = st_ref[...]
    v16 = val.astype(BF16)
    o = _dot_nt((q * jnp.exp2(b)).astype(BF16), st.astype(BF16))
    kd = (k * jnp.exp2(bound - b)).astype(BF16)
    st_ref[...] = st * jnp.exp2(bound) + jnp.dot(val.T.astype(BF16), kd, preferred_element_type=F32)

    lane = lax.broadcasted_iota(jnp.int32, (SUBLANES, LANES), 1)
    m = n // 2
    while m >= SCAN_DIAG:
        groups = n // (2 * m)
        t_off = 0 if reverse else m
        s_off = m if reverse else 0
        r_off = m - 1 if reverse else m
        args = []
        for g in range(groups):
            lo = b_s[g * 2 * m:g * 2 * m + m, :]
            hi = b_s[g * 2 * m + m:(g + 1) * 2 * m, :]
            ref = b_s[g * 2 * m + r_off:g * 2 * m + r_off + 1, :]
            args += [lo - ref, ref - hi] if reverse else [ref - lo, hi - ref]
        ew = jnp.exp2(jnp.concatenate(args, axis=0))
        qe = q_s[...] * ew
        ke = (k_s[...] * ew).astype(BF16)
        qt = jnp.concatenate([qe[g * 2 * m + t_off:g * 2 * m + t_off + m] for g in range(groups)],
                             axis=0).astype(BF16)
        g_all = _dot_nt(qt, ke)
        for g in range(groups):
            t0 = g * 2 * m + t_off
            s0 = g * 2 * m + s_off
            tile0 = (s0 // LANES) * LANES
            if m >= LANES:
                sc_ref[t0:t0 + m, s0:s0 + m] = g_all[g * m:(g + 1) * m, s0:s0 + m]
            else:
                keep = (lane >= s0 - tile0) & (lane < s0 - tile0 + m)
                for i in range(m // SUBLANES):
                    rows = slice(t0 + i * SUBLANES, t0 + (i + 1) * SUBLANES)
                    new = g_all[g * m + i * SUBLANES:g * m + (i + 1) * SUBLANES, tile0:tile0 + LANES]
                    sc_ref[rows, tile0:tile0 + LANES] = jnp.where(
                        keep, new, sc_ref[rows, tile0:tile0 + LANES])
        m //= 2

    o = o + jnp.dot(sc_ref[...].astype(BF16), v16, preferred_element_type=F32)

    trow = lax.broadcasted_iota(jnp.int32, (SCAN_DIAG, LANES), 0)
    diag = []
    for t0 in range(0, n, SCAN_DIAG):
        qb = q_s[t0:t0 + SCAN_DIAG, :]
        bb = b_s[t0:t0 + SCAN_DIAG, :]
        acc = None
        for s in range(SCAN_DIAG):
            ks = k_s[t0 + s:t0 + s + 1, :]
            bs = b_s[t0 + s:t0 + s + 1, :]
            vs = v_s[t0 + s:t0 + s + 1, :]
            w = qb * ks * jnp.exp2(bb - bs)
            if 0 < s < SCAN_DIAG - 1 or (s == 0 and reverse) or (s == SCAN_DIAG - 1 and not reverse):
                w = jnp.where((trow <= s) if reverse else (trow >= s), w, 0.0)
            term = jnp.sum(w, axis=-1, keepdims=True) * vs
            acc = term if acc is None else acc + term
        diag.append(acc)
    o_ref[...] = o + jnp.concatenate(diag, axis=0)


def _scan_kernel(*refs, n_chunks, hb, has_init, want_final, has_mm):
    lb_ref, qf_ref, vf_ref, zf_ref, qb_ref, vb_ref, zb_ref = refs[:7]
    pos = 7
    s0_ref = None
    if has_init:
        s0_ref = refs[pos]
        pos += 1
    if has_mm:
        a_ref, w_ref = refs[pos:pos + 2]
        pos += 2
    of_ref, ob_ref = refs[pos:pos + 2]
    pos += 2
    sfin_ref = None
    if want_final:
        sfin_ref = refs[pos]
        pos += 1
    if has_mm:
        mm_ref = refs[pos]
        pos += 1
    st_ref, sc_ref, tmp_ref = refs[pos:]
    c = pl.program_id(2)

    @pl.when(c == 0)
    def _():
        sc_ref[...] = jnp.zeros(sc_ref.shape, F32)
        for d in range(2):
            for h in range(hb):
                if has_init:
                    st_ref[d, h] = s0_ref[d, h].T
                else:
                    st_ref[d, h] = jnp.zeros((HEAD_DIM, HEAD_DIM), F32)

    for h in range(hb):
        lanes = slice(h * HEAD_DIM, (h + 1) * HEAD_DIM)
        for d, (q_ref, v_ref, z_ref, o_ref) in enumerate(
                ((qf_ref, vf_ref, zf_ref, of_ref), (qb_ref, vb_ref, zb_ref, ob_ref))):
            if has_mm:
                part = a_ref.shape[0] // (2 * hb)
                rows = slice((2 * h + d) * part, (2 * h + d + 1) * part)
                mm_ref[rows, :] = jnp.dot(a_ref[rows, :], w_ref[...], preferred_element_type=F32)
            _scan_chunk(q_ref[:, lanes], z_ref[:, lanes], v_ref[:, lanes],
                        lb_ref[2 * d:2 * d + 1, lanes], lb_ref[2 * d + 1:2 * d + 2, lanes],
                        st_ref.at[d, h], sc_ref.at[d, h], o_ref.at[:, lanes],
                        *(tmp_ref.at[d, h, i] for i in range(4)), reverse=(d == 1))

    if want_final:
        @pl.when(c == n_chunks - 1)
        def _():
            for d in range(2):
                for h in range(hb):
                    sfin_ref[d, h] = st_ref[d, h].T


def _rider_tiles(steps, m, n):
    for tn in (768, 1024, 512, 1536, 2048, 3072, 256, 384):
        if n % tn:
            continue
        n_col = n // tn
        if steps % n_col or m % (steps // n_col):
            continue
        tm = m // (steps // n_col)
        if tm % (2 * SUBLANES) == 0 and tm <= 1024:
            return tm, tn
    return None


def _hgrn_scan(p, lb_terms, state, state_b0, layer, want_final, rider=None):
    bsz, l, _ = p.shape
    d_hg = lb_terms.shape[1]
    heads = d_hg // HEAD_DIM
    hb = SCAN_HEADS_PER_STEP
    w = hb * HEAD_DIM
    n_chunks = l // SCAN_CHUNK
    n_groups = heads // hb
    seg = d_hg // w
    has_init = state is not None

    def fwd(col):
        return pl.BlockSpec((None, SCAN_CHUNK, w), lambda b, g, c: (b, c, col * seg + g))

    def bwd(col):
        return pl.BlockSpec((None, SCAN_CHUNK, w), lambda b, g, c: (b, n_chunks - 1 - c, col * seg + g))

    in_specs = [pl.BlockSpec((4, w), lambda b, g, c: (0, g)),
                fwd(0), fwd(3), fwd(1), bwd(0), bwd(3), bwd(2)]
    args = [lb_terms, p, p, p, p, p, p]
    if has_init:
        in_specs.append(pl.BlockSpec((None, None, 2, hb, HEAD_DIM, HEAD_DIM),
                                     lambda b, g, c: (state_b0 + b, layer, 0, g, 0, 0)))
        args.append(state)
    out_specs = [pl.BlockSpec((None, SCAN_CHUNK, w), lambda b, g, c: (b, c, g)),
                 pl.BlockSpec((None, SCAN_CHUNK, w), lambda b, g, c: (b, n_chunks - 1 - c, g))]
    out_shape = [jax.ShapeDtypeStruct((bsz, l, d_hg), F32)] * 2
    if want_final:
        out_specs.append(pl.BlockSpec((None, 2, hb, HEAD_DIM, HEAD_DIM), lambda b, g, c: (b, 0, g, 0, 0)))
        out_shape.append(jax.ShapeDtypeStruct((bsz, 2, heads, HEAD_DIM, HEAD_DIM), F32))
    if rider is not None:
        a, wr, wl = rider
        m, k = a.shape
        n = wr.shape[2]
        tm, tn = _rider_tiles(bsz * n_groups * n_chunks, m, n)
        n_col = n // tn

        def step(b, g, c):
            return (b * n_groups + g) * n_chunks + c

        in_specs += [pl.BlockSpec((tm, k), lambda b, g, c: (step(b, g, c) // n_col, 0)),
                     pl.BlockSpec((None, k, tn), lambda b, g, c: (wl, 0, step(b, g, c) % n_col))]
        args += [a, wr]
        out_specs.append(pl.BlockSpec((tm, tn), lambda b, g, c: (step(b, g, c) // n_col, step(b, g, c) % n_col)))
        out_shape.append(jax.ShapeDtypeStruct((m, n), F32))
    scratch = [pltpu.VMEM((2, hb, HEAD_DIM, HEAD_DIM), F32),
               pltpu.VMEM((2, hb, SCAN_CHUNK, SCAN_CHUNK), F32),
               pltpu.VMEM((2, hb, 4, SCAN_CHUNK, HEAD_DIM), F32)]
    return pl.pallas_call(
        functools.partial(_scan_kernel, n_chunks=n_chunks, hb=hb, has_init=has_init, want_final=want_final,
                          has_mm=rider is not None),
        grid=(bsz, n_groups, n_chunks),
        in_specs=in_specs,
        out_specs=out_specs,
        out_shape=out_shape,
        scratch_shapes=scratch,
        compiler_params=_params(3),
        name="hgrn_scan",
    )(*args)


def _hg_post_kernel(of_ref, ob_ref, g_ref, ng_ref, o_ref):
    heads = of_ref.shape[-1] // HEAD_DIM
    for h in range(heads):
        lanes = slice(h * HEAD_DIM, (h + 1) * HEAD_DIM)
        o = of_ref[:, lanes] + ob_ref[:, lanes]
        o = o * lax.rsqrt(jnp.mean(o * o, axis=-1, keepdims=True) + EPS) * ng_ref[...]
        o_ref[:, lanes] = (o * _silu(g_ref[:, lanes])).astype(o_ref.dtype)


def _hg_post(o_f, o_b, p, norm_g, tl=256):
    b, l, d_hg = o_f.shape
    blk = pl.BlockSpec((None, tl, d_hg), lambda i, j: (i, j, 0))
    return pl.pallas_call(
        _hg_post_kernel,
        grid=(b, l // tl),
        in_specs=[blk, blk,
                  pl.BlockSpec((None, tl, d_hg), lambda i, j: (i, j, 4)),
                  pl.BlockSpec((1, HEAD_DIM), lambda i, j: (0, 0))],
        out_specs=blk,
        out_shape=jax.ShapeDtypeStruct((b, l, d_hg), BF16),
        compiler_params=_params(2),
        name="hg_post",
    )(o_f, o_b, p, norm_g.reshape(1, HEAD_DIM))


def _conv_seq_kernel(a_ref, b_ref, g_ref, w_ref, bias_ref, lng_ref, lnb_ref, o_ref, pad_ref, y_ref, *, seg):
    tokens, ch = a_ref.shape
    halo = 2 * SUBLANES
    stride = seg + 2 * halo
    sub = min(seg, CONV_ROWS)
    u = a_ref[...] * _sigmoid(b_ref[...])
    for r in range(tokens // seg):
        base = r * stride
        pad_ref[base:base + halo, :] = jnp.zeros((halo, ch), F32)
        pad_ref[base + halo:base + halo + seg, :] = u[r * seg:(r + 1) * seg]
        pad_ref[base + halo + seg:base + stride, :] = jnp.zeros((halo, ch), F32)
    for r in range(tokens // seg):
        for t0 in range(0, seg, sub):
            base = r * stride + halo - CONV_HALF + t0
            for c0 in range(0, ch, LANES):
                acc = jnp.zeros((sub, LANES), F32) + bias_ref[:, c0:c0 + LANES]
                first = base - (base % SUBLANES)
                rows = sub + 2 * halo
                x_al = pad_ref[first:first + rows, c0:c0 + LANES]
                for sh in range(SUBLANES):
                    off = base - first + sh
                    u_sh = x_al if off == 0 else pltpu.roll(x_al, rows - off, axis=0)
                    for a, j in enumerate(range(sh, CONV_TAPS, SUBLANES)):
                        acc = acc + u_sh[a * SUBLANES:a * SUBLANES + sub] * w_ref[j:j + 1, c0:c0 + LANES]
                y_ref[r * seg + t0:r * seg + t0 + sub, c0:c0 + LANES] = acc
    _cv_post_kernel(y_ref, g_ref, lng_ref, lnb_ref, o_ref)


def _conv_seq(p, w, bias, ln_g, ln_b, seg, tokens=256):
    bsz, l, _ = p.shape
    ch = w.shape[1]
    halo = 2 * SUBLANES
    vec = pl.BlockSpec((1, ch), lambda b, t: (0, 0))
    return pl.pallas_call(
        functools.partial(_conv_seq_kernel, seg=seg),
        grid=(bsz, l // tokens),
        in_specs=[pl.BlockSpec((None, tokens, ch), lambda b, t: (b, t, 5)),
                  pl.BlockSpec((None, tokens, ch), lambda b, t: (b, t, 6)),
                  pl.BlockSpec((None, tokens, ch), lambda b, t: (b, t, 7)),
                  pl.BlockSpec((CONV_TAPS, ch), lambda b, t: (0, 0)), vec, vec, vec],
        out_specs=pl.BlockSpec((None, tokens, ch), lambda b, t: (b, t, 0)),
        out_shape=jax.ShapeDtypeStruct((bsz, l, ch), BF16),
        scratch_shapes=[pltpu.VMEM(((tokens // seg) * (seg + 2 * halo), ch), F32),
                        pltpu.VMEM((tokens, ch), F32)],
        compiler_params=_params(2),
        name="conv_seq",
    )(p, p, p, w, bias.reshape(1, ch), ln_g.reshape(1, ch), ln_b.reshape(1, ch))


def _conv_col_kernel(a_ref, b_ref, w_ref, bias_ref, y_ref, pad_ref, *, width):
    tokens, ch = a_ref.shape
    halo = CONV_HALF * width
    pad_ref[0:halo, :] = jnp.zeros((halo, ch), F32)
    pad_ref[halo:halo + tokens, :] = a_ref[...] * _sigmoid(b_ref[...])
    pad_ref[halo + tokens:halo + tokens + halo, :] = jnp.zeros((halo, ch), F32)

    def row(r, carry):
        t0 = pl.multiple_of(r * width, width)
        for c0 in range(0, ch, LANES):
            acc = jnp.zeros((width, LANES), F32) + bias_ref[:, c0:c0 + LANES]
            for j in range(CONV_TAPS):
                acc = acc + pad_ref[pl.ds(t0 + j * width, width), c0:c0 + LANES] * w_ref[j:j + 1, c0:c0 + LANES]
            y_ref[pl.ds(t0, width), c0:c0 + LANES] = acc
        return carry

    lax.fori_loop(0, tokens // width, row, 0)


def _conv_col(p, w, bias, width, tc=256):
    bsz, l, _ = p.shape
    ch = w.shape[1]
    seg_blocks = ch // tc
    return pl.pallas_call(
        functools.partial(_conv_col_kernel, width=width),
        grid=(bsz, ch // tc),
        in_specs=[pl.BlockSpec((None, l, tc), lambda b, c: (b, 0, 5 * seg_blocks + c)),
                  pl.BlockSpec((None, l, tc), lambda b, c: (b, 0, 6 * seg_blocks + c)),
                  pl.BlockSpec((CONV_TAPS, tc), lambda b, c: (0, c)),
                  pl.BlockSpec((1, tc), lambda b, c: (0, c))],
        out_specs=pl.BlockSpec((None, l, tc), lambda b, c: (b, 0, c)),
        out_shape=jax.ShapeDtypeStruct((bsz, l, ch), F32),
        scratch_shapes=[pltpu.VMEM((l + 2 * CONV_HALF * width, tc), F32)],
        compiler_params=_params(2),
        name="conv_col",
    )(p, p, w, bias.reshape(1, ch))


def _cv_post_kernel(y_ref, g_ref, lg_ref, lb_ref, o_ref):
    y = y_ref[...]
    mu = jnp.mean(y, axis=-1, keepdims=True)
    yc = y - mu
    yn = yc * lax.rsqrt(jnp.mean(yc * yc, axis=-1, keepdims=True) + EPS) * lg_ref[...] + lb_ref[...]
    o_ref[...] = (_silu(yn) * _silu(g_ref[...])).astype(o_ref.dtype)


def _cv_post(y, p, ln_g, ln_b, tl=256):
    b, l, ch = y.shape
    blk = pl.BlockSpec((None, tl, ch), lambda i, j: (i, j, 0))
    vec = pl.BlockSpec((1, ch), lambda i, j: (0, 0))
    return pl.pallas_call(
        _cv_post_kernel,
        grid=(b, l // tl),
        in_specs=[blk, pl.BlockSpec((None, tl, ch), lambda i, j: (i, j, 7)), vec, vec],
        out_specs=blk,
        out_shape=jax.ShapeDtypeStruct((b, l, ch), BF16),
        compiler_params=_params(2),
        name="cv_post",
    )(y, p, ln_g.reshape(1, ch), ln_b.reshape(1, ch))


def _layer(streams, layer, mod, lb_terms, norm_g, w_in, w_in_layer, w_in_cast, hg_norm_g, w_dw, b_dw,
           ln_g, ln_b, w_hproj, w_cproj, w_out):
    d = streams[0]["x"].shape[-1]
    d_hg = w_hproj.shape[1]
    hs = [_prenorm(s["x"], s["b0"], s["bsz"], norm_g, mod, s["row"]).reshape(-1, d) for s in streams]
    w_next = None
    if w_in_cast is not None and _cast_slab_ok(hs[0].shape[0], w_in.shape[2]):
        p, w_next = _in_proj(hs[0], w_in, w_in_layer, w_in_cast)
    else:
        p = _in_proj(hs[0], w_in, w_in_layer)
        if w_in_cast is not None:
            w_next = _cast_layer(*w_in_cast)
    new_x, finals = [], []
    scans, ps = [], []
    for i, s in enumerate(streams):
        bsz, l = s["bsz"], s["x"].shape[1]
        rider = None
        if i + 1 < len(streams):
            steps = bsz * (d_hg // (SCAN_HEADS_PER_STEP * HEAD_DIM)) * (l // SCAN_CHUNK)
            if _rider_tiles(steps, hs[i + 1].shape[0], w_in.shape[2]) is not None:
                rider = (hs[i + 1], w_in, w_in_layer)
        res = list(_hgrn_scan(p.reshape(bsz, l, -1), lb_terms, s["state"], s["state_b0"], layer, s["final"],
                              rider))
        ps.append(p)
        scans.append(res[:2])
        finals.append(res[2] if s["final"] else None)
        if rider is not None:
            p = res[-1]
        elif i + 1 < len(streams):
            p = _in_proj(hs[i + 1], w_in, w_in_layer)
    for s, p, (o_f, o_b) in zip(streams, ps, scans):
        bsz, l = s["bsz"], s["x"].shape[1]
        p3 = p.reshape(bsz, l, -1)
        o_hg = _hg_post(o_f, o_b, p3, hg_norm_g)
        if s["conv"] == "col":
            o_cv = _cv_post(_conv_col(p3, w_dw, b_dw, LATENT_GRID_W), p3, ln_g, ln_b)
        else:
            o_cv = _conv_seq(p3, w_dw, b_dw, ln_g, ln_b, LATENT_GRID_W if s["conv"] == "row" else l)
        merged = _merge_proj(o_hg.reshape(bsz * l, d_hg), w_hproj, o_cv.reshape(bsz * l, -1), w_cproj,
                             layer, p, 4 * d)
        x_new = _out_proj(merged, w_out, layer, s["x"].reshape(-1, d), s["b0"] * l, mod, l, s["row"])
        new_x.append(x_new.reshape(bsz, l, d))
    return new_x, finals, w_next


def kernel(x_prompt, x_sample, state_hgrn, c, c_ctx, mod_w, mod_b, norm_g, w_in, hg_lb_logits, hg_norm_g,
           cv_dw_w, cv_dw_b, cv_ln_g, cv_ln_b, w_hproj, w_cproj, w_out, final_g):
    depth = mod_w.shape[0]
    d = x_prompt.shape[-1]
    dec_batch = c.shape[0]
    rows = 1 + dec_batch
    pad = (-rows) % SUBLANES
    cond = jnp.concatenate([c_ctx.reshape(1, d), c, jnp.zeros((pad, d), F32)], axis=0).astype(F32)
    mod = _modulation(cond, mod_w, mod_b)
    mod = mod.reshape(depth, rows + pad, 1, 3 * d)
    lb = _lower_bound_terms(hg_lb_logits)
    d_hg = hg_lb_logits.shape[-1]
    lb = lb.reshape(2, depth, 2, d_hg)

    w_in16 = _cast_layer(w_in, 0)
    w_h16 = w_hproj.astype(BF16)
    w_c16 = w_cproj.astype(BF16)
    w_o16 = w_out.astype(BF16)

    parts = 2 if dec_batch % 2 == 0 else 1
    per = dec_batch // parts
    streams = []
    for i in range(parts):
        streams.append(dict(x=x_sample, b0=i * per, bsz=per, row=functools.partial(lambda b, o: 1 + o + b, o=i * per),
                            state=state_hgrn, state_b0=i * per, final=False, conv=None))
    streams.append(dict(x=x_prompt, b0=0, bsz=x_prompt.shape[0], row=lambda b: 0, state=None, state_b0=0,
                        final=True, conv=None))
    states = []
    for l in range(depth):
        lb_l = jnp.transpose(lb[:, l], (1, 0, 2)).reshape(4, d_hg)
        for s in streams[:parts]:
            s["conv"] = "row" if l % 2 == 0 else "col"
        xs, finals, w_next = _layer(streams, l, mod[l], lb_l, norm_g[l], w_in16, 0,
                                    (w_in, l + 1) if l + 1 < depth else None, hg_norm_g[l], cv_dw_w[l],
                                    cv_dw_b[l], cv_ln_g[l], cv_ln_b[l], w_h16, w_c16, w_o16)
        w_in16 = w_next
        for s, x in zip(streams, xs):
            s["x"], s["b0"] = x, 0
        states.append(finals[-1])
    y_prompt = _final_norm(streams[-1]["x"], final_g)
    if parts == 2:
        y_sample = _final_norm_pair(streams[0]["x"], streams[1]["x"], final_g)
    else:
        y_sample = _final_norm(streams[0]["x"], final_g)
    return y_prompt, y_sample, jnp.stack(states, axis=1)
```
